```python
import jax, jax.numpy as jnp
from jax import lax
import numpy as np

D_MODEL = 1024
BATCH = 8
SEQ = 4096
DEPTH = 1

D_MIX = D_MODEL
N_HEADS = 8
HEAD_DIM = 64
N_KV_HEADS = 2
Q_PER_KV = N_HEADS // N_KV_HEADS
D_ATTN = N_HEADS * HEAD_DIM
IDX_HEADS = 8
IDX_DIM = 64
TOPK_MAX = 256
Q_BLOCK = 128
POOL_WINDOWS = (2, 4, 8, 16)
N_POOL_GROUPS = len(POOL_WINDOWS)
D_POOL = D_MIX - D_ATTN
POOL_GROUP_DIM = D_POOL // N_POOL_GROUPS

Q_COLS = N_HEADS * HEAD_DIM
KV_COLS = N_KV_HEADS * HEAD_DIM
QI_COLS = IDX_HEADS * IDX_DIM
KI_COLS = IDX_DIM
WI_COLS = IDX_HEADS
SPLIT_POINTS = (Q_COLS, Q_COLS + KV_COLS, Q_COLS + 2 * KV_COLS, Q_COLS + 2 * KV_COLS + D_POOL,
                Q_COLS + 2 * KV_COLS + D_POOL + QI_COLS, Q_COLS + 2 * KV_COLS + D_POOL + QI_COLS + KI_COLS)
IN_COLS = Q_COLS + 2 * KV_COLS + D_POOL + QI_COLS + KI_COLS + WI_COLS

N_EXPERT_GROUPS = 4
EXPERTS_PER_GROUP = 4
TOP_K_INNER = 2
D_EXPERT = 256

ALPHA = float((2 * DEPTH) ** 0.25)
BETA = float((8 * DEPTH) ** -0.25)
LN_EPS = 1e-5

kernel_name = "hymba_dsa_pool_hmoe_deepnorm"


def _layer_norm(z, g, b):
    z32 = z.astype(jnp.float32)
    mu = jnp.mean(z32, axis=-1, keepdims=True)
    var = jnp.mean(jnp.square(z32 - mu), axis=-1, keepdims=True)
    y = (z32 - mu) * lax.rsqrt(var + LN_EPS) * g.astype(jnp.float32) + b.astype(jnp.float32)
    return y.astype(z.dtype)


def _dsa_attention(q, k, v, q_idx, k_idx, w_idx):
    B, S = q.shape[0], q.shape[1]
    topk = min(TOPK_MAX, S // 4)
    n_blk = S // Q_BLOCK
    key_pos = jnp.arange(S)
    idx_scale = (IDX_DIM ** -0.5) * (IDX_HEADS ** -0.5)
    attn_scale = HEAD_DIM ** -0.5
    k_idx32 = k_idx.astype(jnp.float32)

    def to_blocks(a):
        return jnp.swapaxes(a.reshape((B, n_blk, Q_BLOCK) + a.shape[2:]), 0, 1)

    def block_fn(args):
        qb, qib, wib, start = args
        q_pos = start + jnp.arange(Q_BLOCK)
        causal = key_pos[None, :] <= q_pos[:, None]
        logits = jnp.einsum('bqhd,bsd->bqhs', qib.astype(jnp.float32), k_idx32)
        score = jnp.einsum('bqhs,bqh->bqs', jax.nn.relu(logits), wib.astype(jnp.float32)) * idx_scale
        score = jnp.where(causal[None], score, -jnp.inf)
        _, sel = lax.top_k(score, topk)
        valid = sel <= q_pos[None, :, None]
        k_sel = jax.vmap(lambda kb, ib: kb[ib])(k, sel)
        v_sel = jax.vmap(lambda vb, ib: vb[ib])(v, sel)
        qg = qb.reshape(B, Q_BLOCK, N_KV_HEADS, Q_PER_KV, HEAD_DIM)
        s = jnp.einsum('bqkgd,bqnkd->bqkgn', qg, k_sel).astype(jnp.float32) * attn_scale
        s = jnp.where(valid[:, :, None, None, :], s, -jnp.inf)
        p = jax.nn.softmax(s, axis=-1)
        o = jnp.einsum('bqkgn,bqnkd->bqkgd', p.astype(v.dtype), v_sel)
        return o.reshape(B, Q_BLOCK, N_HEADS * HEAD_DIM)

    starts = jnp.arange(n_blk) * Q_BLOCK
    out = lax.map(block_fn, (to_blocks(q), to_blocks(q_idx), to_blocks(w_idx), starts))
    return jnp.swapaxes(out, 0, 1).reshape(B, S, N_HEADS * HEAD_DIM)


def _multiscale_pool(u, w_pool, pool_scale):
    B, S = u.shape[0], u.shape[1]
    u32 = u.astype(jnp.float32).reshape(B, S, N_POOL_GROUPS, POOL_GROUP_DIM)
    pos = jnp.arange(S)
    outs = []
    for g, win in enumerate(POOL_WINDOWS):
        ug = u32[:, :, g]
        c = jnp.cumsum(ug, axis=1)
        c_lag = jnp.pad(c, ((0, 0), (win, 0), (0, 0)))[:, :S]
        count = jnp.minimum(pos + 1, win).astype(jnp.float32)[None, :, None]
        outs.append((c - c_lag) / count - ug)
    d = jnp.stack(outs, axis=2)
    y = jnp.einsum('bsgc,gcd->bsgd', d, w_pool.astype(jnp.float32)).reshape(B, S, D_POOL)
    return (y * pool_scale.astype(jnp.float32)).astype(u.dtype)


def _hier_moe(h, w_group_router, b_group_router, w_expert_router, b_expert_router, w_gate, w_up, w_down):
    B, S, D = h.shape
    t = h.reshape(B * S, D)
    t32 = t.astype(jnp.float32)
    g_logits = t32 @ w_group_router.astype(jnp.float32) + b_group_router.astype(jnp.float32)
    g_prob = jax.nn.softmax(g_logits, axis=-1)
    g_sel = jnp.argmax(g_logits, axis=-1)
    g_weight = jnp.take_along_axis(g_prob, g_sel[:, None], axis=-1)
    e_logits = (t32 @ w_expert_router.astype(jnp.float32) + b_expert_router.astype(jnp.float32))
    e_logits = e_logits.reshape(-1, N_EXPERT_GROUPS, EXPERTS_PER_GROUP)
    e_sel_logits = jnp.take_along_axis(e_logits, g_sel[:, None, None], axis=1)[:, 0]
    top_val, top_idx = lax.top_k(e_sel_logits, TOP_K_INNER)
    top_w = jax.nn.softmax(top_val, axis=-1) * g_weight
    inner_gate = jnp.sum(jax.nn.one_hot(top_idx, EXPERTS_PER_GROUP, dtype=jnp.float32) * top_w[..., None], axis=1)
    gate = jax.nn.one_hot(g_sel, N_EXPERT_GROUPS, dtype=jnp.float32)[:, :, None] * inner_gate[:, None, :]
    y = jnp.zeros((B * S, D), jnp.float32)
    for g in range(N_EXPERT_GROUPS):
        a = jnp.einsum('td,edf->tef', t, w_gate[g])
        b = jnp.einsum('td,edf->tef', t, w_up[g])
        hid = jax.nn.silu(a) * b * gate[:, g, :, None].astype(t.dtype)
        y = y + jnp.einsum('tef,efd->td', hid, w_down[g]).astype(jnp.float32)
    return y.reshape(B, S, D).astype(h.dtype)


def setup_inputs(seed: int = 0) -> dict:
    key = jax.random.key(seed)
    ks = jax.random.split(key, 16)
    f32 = jnp.float32
    nrm = jax.random.normal
    L = DEPTH
    x = nrm(ks[0], (BATCH, SEQ, D_MODEL), f32)
    col_scale = jnp.concatenate([
        jnp.ones((Q_COLS + KV_COLS,), f32),
        jnp.full((KV_COLS + D_POOL,), BETA, f32),
        jnp.ones((QI_COLS + KI_COLS + WI_COLS,), f32)])
    w_in = nrm(ks[1], (L, D_MODEL, IN_COLS), f32) * (D_MODEL ** -0.5) * col_scale
    w_pool = nrm(ks[2], (L, N_POOL_GROUPS, POOL_GROUP_DIM, POOL_GROUP_DIM), f32) * (POOL_GROUP_DIM ** -0.5)
    pool_scale = 1.0 + 0.02 * nrm(ks[3], (L, D_POOL), f32)
    w_out = nrm(ks[4], (L, D_MIX, D_MODEL), f32) * (D_MIX ** -0.5) * BETA
    ln1_g = 1.0 + 0.02 * nrm(ks[5], (L, D_MODEL), f32)
    ln1_b = 0.02 * nrm(ks[6], (L, D_MODEL), f32)
    w_group_router = nrm(ks[7], (L, D_MODEL, N_EXPERT_GROUPS), f32) * (D_MODEL ** -0.5)
    b_group_router = 0.01 * nrm(ks[8], (L, N_EXPERT_GROUPS), f32)
    w_expert_router = nrm(ks[9], (L, D_MODEL, N_EXPERT_GROUPS * EXPERTS_PER_GROUP), f32) * (D_MODEL ** -0.5)
    b_expert_router = 0.01 * nrm(ks[10], (L, N_EXPERT_GROUPS * EXPERTS_PER_GROUP), f32)
    w_gate = nrm(ks[11], (L, N_EXPERT_GROUPS, EXPERTS_PER_GROUP, D_MODEL, D_EXPERT), f32) * (D_MODEL ** -0.5)
    w_up = nrm(ks[12], (L, N_EXPERT_GROUPS, EXPERTS_PER_GROUP, D_MODEL, D_EXPERT), f32) * (D_MODEL ** -0.5)
    w_down = nrm(ks[13], (L, N_EXPERT_GROUPS, EXPERTS_PER_GROUP, D_EXPERT, D_MODEL), f32) * (D_EXPERT ** -0.5) * BETA
    ln2_g = 1.0 + 0.02 * nrm(ks[14], (L, D_MODEL), f32)
    ln2_b = 0.02 * nrm(ks[15], (L, D_MODEL), f32)
    return {"x": x, "w_in": w_in, "w_pool": w_pool, "pool_scale": pool_scale, "w_out": w_out,
            "ln1_g": ln1_g, "ln1_b": ln1_b, "w_group_router": w_group_router,
            "b_group_router": b_group_router, "w_expert_router": w_expert_router,
            "b_expert_router": b_expert_router, "w_gate": w_gate, "w_up": w_up, "w_down": w_down,
            "ln2_g": ln2_g, "ln2_b": ln2_b}


def reference(x, w_in, w_pool, pool_scale, w_out, ln1_g, ln1_b, w_group_router, b_group_router,
              w_expert_router, b_expert_router, w_gate, w_up, w_down, ln2_g, ln2_b):
    B, S, _ = x.shape
    for l in range(DEPTH):
        proj = jnp.einsum('bsd,dc->bsc', x, w_in[l])
        q, k, v, pool_in, q_idx, k_idx, w_idx = jnp.split(proj, SPLIT_POINTS, axis=-1)
        q = q.reshape(B, S, N_HEADS, HEAD_DIM)
        k = k.reshape(B, S, N_KV_HEADS, HEAD_DIM)
        v = v.reshape(B, S, N_KV_HEADS, HEAD_DIM)
        q_idx = q_idx.reshape(B, S, IDX_HEADS, IDX_DIM)
        attn_out = _dsa_attention(q, k, v, q_idx, k_idx, w_idx)
        pool_out = _multiscale_pool(pool_in, w_pool[l], pool_scale[l])
        mix = jnp.einsum('bsc,cd->bsd', jnp.concatenate([attn_out, pool_out], axis=-1), w_out[l])
        x = _layer_norm(ALPHA * x + mix, ln1_g[l], ln1_b[l])
        ffn = _hier_moe(x, w_group_router[l], b_group_router[l], w_expert_router[l], b_expert_router[l],
                        w_gate[l], w_up[l], w_down[l])
        x = _layer_norm(ALPHA * x + ffn, ln2_g[l], ln2_b[l])
    return x
```

```python
import functools

import jax
import jax.numpy as jnp
from jax import lax
from jax.experimental import pallas as pl
from jax.experimental.pallas import tpu as pltpu

D_MODEL = 1024
N_HEADS = 8
HEAD_DIM = 64
N_KV_HEADS = 2
Q_PER_KV = N_HEADS // N_KV_HEADS
IDX_HEADS = 8
IDX_DIM = 64
TOPK_MAX = 256
POOL_WINDOWS = (2, 4, 8, 16)
N_POOL_GROUPS = len(POOL_WINDOWS)
POOL_GROUP_DIM = 128
D_POOL = N_POOL_GROUPS * POOL_GROUP_DIM
MAX_WIN = max(POOL_WINDOWS)
Q_COLS = N_HEADS * HEAD_DIM
KV_COLS = N_KV_HEADS * HEAD_DIM
QI_COLS = IDX_HEADS * IDX_DIM
N_EXPERT_GROUPS = 4
EXPERTS_PER_GROUP = 4
N_EXPERTS = N_EXPERT_GROUPS * EXPERTS_PER_GROUP
D_EXPERT = 256
DEPTH = 1
ALPHA = float((2 * DEPTH) ** 0.25)
LN_EPS = 1e-5

LANES = 128
SUBLANES = 8
MXU_DIM = 256
VMEM_LIMIT_BYTES = 56 * 1024 * 1024

TM_PROJ = 512
TQ = MXU_DIM
TK = MXU_DIM
TM_OUT = 512
TM_MOE = 512

F32_MAX = float(jnp.finfo(jnp.float32).max)
BISECT_MAX_ITERS = 48
TIE_INDEX_ITERS = 13

_NT_DIMS = (((1,), (1,)), ((), ()))


def _dot(a, b):
    return jnp.dot(a, b, preferred_element_type=jnp.float32)


def _dot_nt(a, b):
    return lax.dot_general(a, b, _NT_DIMS, preferred_element_type=jnp.float32)


def _proj_kernel(x_ref, wqT_ref, wvT_ref, wqiT_ref, wwiT_ref, wk0_ref, wk1_ref, wki_ref, wpin_ref,
                 wpool_ref, pscale_ref,
                 qT_ref, vT_ref, qiT_ref, wiT_ref, k0_ref, k1_ref, ki_ref, pool_ref,
                 hist_ref, ext_ref, *, idx_scale):
    s_blk = pl.program_id(1)
    tm = x_ref.shape[1]
    xb = x_ref[0].astype(jnp.bfloat16)

    qT_ref[0] = _dot_nt(wqT_ref[...], xb).astype(jnp.bfloat16)
    qiT_ref[0] = _dot_nt(wqiT_ref[...], xb).astype(jnp.bfloat16)
    wiT_ref[0] = _dot_nt(wwiT_ref[...], xb)[:IDX_HEADS] * idx_scale
    vT = _dot_nt(wvT_ref[...], xb).astype(jnp.bfloat16)
    for j in range(tm // TK):
        vT_ref[0, j] = vT[:, j * TK:(j + 1) * TK]

    k0_ref[0] = _dot(xb, wk0_ref[...]).astype(jnp.bfloat16)
    k1_ref[0] = _dot(xb, wk1_ref[...]).astype(jnp.bfloat16)
    ki_ref[0] = _dot(xb, wki_ref[...]).astype(jnp.bfloat16)

    u = _dot(xb, wpin_ref[...])

    @pl.when(s_blk == 0)
    def _():
        hist_ref[...] = jnp.zeros_like(hist_ref)

    ext_ref[0:MAX_WIN, :] = hist_ref[...]
    ext_ref[MAX_WIN:MAX_WIN + tm, :] = u
    hist_ref[...] = u[tm - MAX_WIN:tm, :]

    pos = s_blk * tm + lax.broadcasted_iota(jnp.int32, (tm, POOL_GROUP_DIM), 0)
    for g, win in enumerate(POOL_WINDOWS):
        c0 = g * POOL_GROUP_DIM
        acc = ext_ref[MAX_WIN:MAX_WIN + tm, c0:c0 + POOL_GROUP_DIM]
        for j in range(1, win):
            acc = acc + ext_ref[MAX_WIN - j:MAX_WIN - j + tm, c0:c0 + POOL_GROUP_DIM]
        count = jnp.minimum(pos + 1, win).astype(jnp.float32)
        d = acc / count - ext_ref[MAX_WIN:MAX_WIN + tm, c0:c0 + POOL_GROUP_DIM]
        y = _dot(d.astype(jnp.bfloat16), wpool_ref[g]) * pscale_ref[:, c0:c0 + POOL_GROUP_DIM]
        pool_ref[0, :, c0:c0 + POOL_GROUP_DIM] = y.astype(jnp.bfloat16)


def _input_projection(x, w_in, w_pool, pool_scale):
    B, S, D = x.shape
    tm = TM_PROJ
    bf = jnp.bfloat16
    idx_scale = (IDX_DIM ** -0.5) * (IDX_HEADS ** -0.5)
    attn_scale = HEAD_DIM ** -0.5

    o = 0
    w_q = w_in[:, o:o + Q_COLS]; o += Q_COLS
    w_k = w_in[:, o:o + KV_COLS]; o += KV_COLS
    w_v = w_in[:, o:o + KV_COLS]; o += KV_COLS
    w_pin = w_in[:, o:o + D_POOL]; o += D_POOL
    w_qi = w_in[:, o:o + QI_COLS]; o += QI_COLS
    w_ki = w_in[:, o:o + IDX_DIM]; o += IDX_DIM
    w_wi = w_in[:, o:o + IDX_HEADS]

    wqT = (w_q * attn_scale).T.astype(bf)
    wvT = w_v.T.astype(bf)
    wqiT = w_qi.T.astype(bf)
    wwiT = jnp.pad(w_wi.T, ((0, 2 * SUBLANES - IDX_HEADS), (0, 0))).astype(bf)
    wk0 = w_k[:, :HEAD_DIM].astype(bf)
    wk1 = w_k[:, HEAD_DIM:].astype(bf)

    full = lambda a: pl.BlockSpec(a.shape, lambda b, s: (0,) * a.ndim)
    weights = (wqT, wvT, wqiT, wwiT, wk0, wk1, w_ki.astype(bf), w_pin.astype(bf),
               w_pool.astype(bf), pool_scale.reshape(1, D_POOL))
    n_chunk = S // TK
    out_shape = (
        jax.ShapeDtypeStruct((B, Q_COLS, S), bf),
        jax.ShapeDtypeStruct((B, n_chunk, KV_COLS, TK), bf),
        jax.ShapeDtypeStruct((B, QI_COLS, S), bf),
        jax.ShapeDtypeStruct((B, IDX_HEADS, S), jnp.float32),
        jax.ShapeDtypeStruct((B, S, HEAD_DIM), bf),
        jax.ShapeDtypeStruct((B, S, HEAD_DIM), bf),
        jax.ShapeDtypeStruct((B, S, IDX_DIM), bf),
        jax.ShapeDtypeStruct((B, S, D_POOL), bf),
    )
    out_specs = (
        pl.BlockSpec((1, Q_COLS, tm), lambda b, s: (b, 0, s)),
        pl.BlockSpec((1, tm // TK, KV_COLS, TK), lambda b, s: (b, s, 0, 0)),
        pl.BlockSpec((1, QI_COLS, tm), lambda b, s: (b, 0, s)),
        pl.BlockSpec((1, IDX_HEADS, tm), lambda b, s: (b, 0, s)),
        pl.BlockSpec((1, tm, HEAD_DIM), lambda b, s: (b, s, 0)),
        pl.BlockSpec((1, tm, HEAD_DIM), lambda b, s: (b, s, 0)),
        pl.BlockSpec((1, tm, IDX_DIM), lambda b, s: (b, s, 0)),
        pl.BlockSpec((1, tm, D_POOL), lambda b, s: (b, s, 0)),
    )
    return pl.pallas_call(
        functools.partial(_proj_kernel, idx_scale=idx_scale),
        grid=(B, S // tm),
        in_specs=[pl.BlockSpec((1, tm, D), lambda b, s: (b, s, 0))] + [full(w) for w in weights],
        out_specs=out_specs,
        out_shape=out_shape,
        scratch_shapes=[pltpu.VMEM((MAX_WIN, D_POOL), jnp.float32),
                        pltpu.VMEM((MAX_WIN + tm, D_POOL), jnp.float32)],
        compiler_params=pltpu.CompilerParams(
            dimension_semantics=("arbitrary", "arbitrary"), vmem_limit_bytes=VMEM_LIMIT_BYTES),
        name="dsa_input_projection",
    )(x, *weights)


def _dsa_kernel(qT_ref, qiT_ref, wiT_ref, k0_ref, k1_ref, ki_ref, vT_ref, out_ref,
                scr_ref, m_ref, l_ref, acc_ref):
    q_blk = pl.program_id(1)
    n_chunks = q_blk + 1
    q0 = q_blk * TQ
    topk = float(TOPK_MAX)
    neg_inf = -jnp.inf

    qpos = q0 + lax.broadcasted_iota(jnp.int32, (1, TQ), 1)
    kiota = lax.broadcasted_iota(jnp.int32, (TK, TQ), 0)

    def score_chunk(c, carry):
        mn, mx = carry
        kic = ki_ref[0, c]
        sc = jnp.zeros((TK, TQ), jnp.float32)
        for h in range(IDX_HEADS):
            z = _dot(kic, qiT_ref[0, h * IDX_DIM:(h + 1) * IDX_DIM, :])
            sc = sc + jnp.maximum(z, 0.0) * wiT_ref[0, h:h + 1, :]
        valid = (c * TK + kiota) <= qpos
        scr_ref[c] = jnp.where(valid, sc, neg_inf)
        mx = jnp.maximum(mx, jnp.max(jnp.where(valid, sc, neg_inf), axis=0, keepdims=True))
        mn = jnp.minimum(mn, jnp.min(jnp.where(valid, sc, jnp.inf), axis=0, keepdims=True))
        return mn, mx

    row = lambda v: jnp.full((1, TQ), v, jnp.float32)
    smin, smax = lax.fori_loop(0, n_chunks, score_chunk, (row(jnp.inf), row(neg_inf)))

    def reduce_chunks(fn, init):
        def body(c, acc):
            part = fn(c, scr_ref[c])
            return acc + jnp.sum(part.reshape(TK // SUBLANES, SUBLANES, TQ), axis=0)
        acc = lax.fori_loop(0, n_chunks, body, jnp.zeros((SUBLANES, TQ), jnp.float32))
        return jnp.sum(acc, axis=0, keepdims=True)

    def count_ge(p):
        return reduce_chunks(lambda c, blk: jnp.where(blk >= p, 1.0, 0.0), None)

    n_valid = (qpos + 1).astype(jnp.float32)
    done0 = jnp.where(n_valid <= topk, 1.0, 0.0)
    thr0 = row(-F32_MAX)

    def n_open(done, stuck):
        return jnp.sum(jnp.where(done + stuck > 0.0, 0.0, 1.0))

    def bisect_cond(st):
        it, n_act = st[0], st[1]
        return jnp.logical_and(it < BISECT_MAX_ITERS, n_act > 0.0)

    def bisect_body(st):
        it, _, lo, hi, c_lo, c_hi, thr, done, stuck = st
        hi_f = jnp.minimum(hi, smax)
        mid = 0.5 * lo + 0.5 * hi_f
        frac = (c_lo - (topk - 0.5)) / jnp.maximum(c_lo - c_hi, 1.0)
        p_int = lo + (hi_f - lo) * frac
        use_int = jnp.where((it % 2) == 1, 1.0, 0.0)
        inside = jnp.where(p_int > lo, jnp.where(p_int < hi, 1.0, 0.0), 0.0)
        p = jnp.where(use_int * inside > 0.0, p_int, mid)
        mid_ok = jnp.where(mid > lo, jnp.where(mid < hi, 1.0, 0.0), 0.0)
        stuck = jnp.maximum(stuck, 1.0 - mid_ok)
        active = jnp.where(done + stuck > 0.0, 0.0, 1.0)
        c = count_ge(p)
        is_eq = active * jnp.where(c == topk, 1.0, 0.0)
        is_gt = active * jnp.where(c > topk, 1.0, 0.0)
        is_lt = active * jnp.where(c < topk, 1.0, 0.0)
        thr = jnp.where(is_eq > 0.0, p, thr)
        done = jnp.maximum(done, is_eq)
        lo = jnp.where(is_gt > 0.0, p, lo)
        c_lo = jnp.where(is_gt > 0.0, c, c_lo)
        hi = jnp.where(is_lt > 0.0, p, hi)
        c_hi = jnp.where(is_lt > 0.0, c, c_hi)
        return (it + 1, n_open(done, stuck), lo, hi, c_lo, c_hi, thr, done, stuck)

    st = (jnp.int32(0), n_open(done0, row(0.0)), smin, row(jnp.inf), n_valid, row(0.0),
          thr0, done0, row(0.0))
    st = lax.while_loop(bisect_cond, bisect_body, st)
    _, _, _, hi, _, c_hi, thr, done, _ = st

    def walk_cond(st):
        return st[0] > 0.0

    def walk_body(st):
        _, hi, c_hi, thr, done, need = st
        def below_max(c, acc):
            blk = scr_ref[c]
            part = jnp.where(blk < hi, blk, neg_inf)
            return jnp.maximum(acc, jnp.max(part.reshape(TK // SUBLANES, SUBLANES, TQ), axis=0))
        m8 = lax.fori_loop(0, n_chunks, below_max, jnp.full((SUBLANES, TQ), neg_inf, jnp.float32))
        m = jnp.max(m8, axis=0, keepdims=True)
        c = count_ge(m)
        active = 1.0 - done
        reach = active * jnp.where(c >= topk, 1.0, 0.0)
        thr = jnp.where(reach > 0.0, m, thr)
        need = jnp.where(reach > 0.0, topk - c_hi, need)
        done = jnp.maximum(done, reach)
        step = active * (1.0 - reach)
        hi = jnp.where(step > 0.0, m, hi)
        c_hi = jnp.where(step > 0.0, c, c_hi)
        return (jnp.sum(1.0 - done), hi, c_hi, thr, done, need)

    def resolve_ties():
        st = (jnp.sum(1.0 - done), hi, c_hi, thr, done, row(0.0))
        _, _, _, thr_w, _, need = lax.while_loop(walk_cond, walk_body, st)
        walked = 1.0 - done
        def cut_body(_, st):
            lo_j, hi_j = st
            mid_j = jnp.floor((lo_j + hi_j) * 0.5)
            cnt = reduce_chunks(
                lambda c, blk: jnp.where(
                    blk == thr_w,
                    jnp.where((c * TK + kiota).astype(jnp.float32) <= mid_j, 1.0, 0.0), 0.0), None)
            ok = jnp.where(cnt >= need, 1.0, 0.0)
            return (jnp.where(ok > 0.0, lo_j, mid_j), jnp.where(ok > 0.0, mid_j, hi_j))
        last = (n_chunks * TK - 1).astype(jnp.float32)
        _, cut = lax.fori_loop(0, TIE_INDEX_ITERS, cut_body, (row(-1.0), row(0.0) + last))
        def drop(c, carry):
            blk = scr_ref[c]
            kill = jnp.where(
                blk == thr_w,
                jnp.where((c * TK + kiota).astype(jnp.float32) > cut, walked, 0.0), 0.0)
            scr_ref[c] = jnp.where(kill > 0.0, neg_inf, blk)
            return carry
        lax.fori_loop(0, n_chunks, drop, 0)
        return thr_w

    thr = lax.cond(jnp.sum(1.0 - done) > 0.0, resolve_ties, lambda: thr)

    m_ref[...] = jnp.full_like(m_ref, neg_inf)
    l_ref[...] = jnp.zeros_like(l_ref)
    acc_ref[...] = jnp.zeros_like(acc_ref)

    def attend_chunk(c, carry):
        bias = jnp.where(scr_ref[c] >= thr, 0.0, neg_inf)
        kcs = (k0_ref[0, c], k1_ref[0, c])
        for h in range(N_HEADS):
            g = h // Q_PER_KV
            s = _dot(kcs[g], qT_ref[0, h * HEAD_DIM:(h + 1) * HEAD_DIM, :]) + bias
            m_old = m_ref[h:h + 1, :]
            m_new = jnp.maximum(m_old, jnp.max(s, axis=0, keepdims=True))
            m_safe = jnp.where(m_new == neg_inf, 0.0, m_new)
            corr = jnp.exp(m_old - m_safe)
            p = jnp.exp(s - m_safe)
            l_ref[h:h + 1, :] = l_ref[h:h + 1, :] * corr + jnp.sum(p, axis=0, keepdims=True)
            vTc = vT_ref[0, c, g * HEAD_DIM:(g + 1) * HEAD_DIM, :]
            acc_ref[h] = acc_ref[h] * corr + _dot(vTc, p.astype(jnp.bfloat16))
            m_ref[h:h + 1, :] = m_new
        return carry

    lax.fori_loop(0, n_chunks, attend_chunk, 0)

    for h in range(N_HEADS):
        acc_ref[h] = acc_ref[h] / l_ref[h:h + 1, :]
    outT = acc_ref[...].reshape(N_HEADS * HEAD_DIM, TQ)
    out_ref[0] = outT.T.astype(jnp.bfloat16)


def _dsa_attention(qT, qiT, wiT, k0, k1, ki, vT):
    B, _, S = qT.shape
    n_chunk = S // TK
    k0 = k0.reshape(B, n_chunk, TK, HEAD_DIM)
    k1 = k1.reshape(B, n_chunk, TK, HEAD_DIM)
    ki = ki.reshape(B, n_chunk, TK, IDX_DIM)
    per_batch = lambda a: pl.BlockSpec((1,) + a.shape[1:], lambda b, i: (b,) + (0,) * (a.ndim - 1))
    return pl.pallas_call(
        _dsa_kernel,
        grid=(B, S // TQ),
        in_specs=[
            pl.BlockSpec((1, Q_COLS, TQ), lambda b, i: (b, 0, i)),
            pl.BlockSpec((1, QI_COLS, TQ), lambda b, i: (b, 0, i)),
            pl.BlockSpec((1, IDX_HEADS, TQ), lambda b, i: (b, 0, i)),
            per_batch(k0), per_batch(k1), per_batch(ki), per_batch(vT),
        ],
        out_specs=pl.BlockSpec((1, TQ, Q_COLS), lambda b, i: (b, i, 0)),
        out_shape=jax.ShapeDtypeStruct((B, S, Q_COLS), jnp.bfloat16),
        scratch_shapes=[
            pltpu.VMEM((n_chunk, TK, TQ), jnp.float32),
            pltpu.VMEM((N_HEADS, TQ), jnp.float32),
            pltpu.VMEM((N_HEADS, TQ), jnp.float32),
            pltpu.VMEM((N_HEADS, HEAD_DIM, TQ), jnp.float32),
        ],
        compiler_params=pltpu.CompilerParams(
            dimension_semantics=("arbitrary", "arbitrary"), vmem_limit_bytes=VMEM_LIMIT_BYTES),
        name="dsa_attention",
    )(qT, qiT, wiT, k0, k1, ki, vT)


def _layer_norm(z, g, b):
    mu = jnp.mean(z, axis=-1, keepdims=True)
    zc = z - mu
    var = jnp.mean(zc * zc, axis=-1, keepdims=True)
    return zc * lax.rsqrt(var + LN_EPS) * g + b


def _out_kernel(attn_ref, pool_ref, x_ref, woa_ref, wop_ref, g_ref, b_ref, wr_ref, br_ref,
                x1_ref, x1b_ref, gate_ref):
    mix = _dot(attn_ref[...], woa_ref[...]) + _dot(pool_ref[...], wop_ref[...])
    x1 = _layer_norm(ALPHA * x_ref[...] + mix, g_ref[...], b_ref[...])
    x1_ref[...] = x1
    hi = x1.astype(jnp.bfloat16)
    x1b_ref[...] = hi
    lo = (x1 - hi.astype(jnp.float32)).astype(jnp.bfloat16)

    r = _dot(hi, wr_ref[...])
    logits = r[:, :LANES] + r[:, LANES:] + _dot(lo, wr_ref[:, :LANES]) + br_ref[...]

    tm = logits.shape[0]
    lane = lax.broadcasted_iota(jnp.int32, (tm, LANES), 1)
    neg_inf = -jnp.inf
    is_grp = jnp.where(lane >= N_EXPERTS, jnp.where(lane < N_EXPERTS + N_EXPERT_GROUPS, 1.0, 0.0), 0.0)
    gl = jnp.where(is_grp > 0.0, logits, neg_inf)
    g_max = jnp.max(gl, axis=1, keepdims=True)
    g_sel = jnp.min(jnp.where(gl == g_max, lane, LANES), axis=1, keepdims=True) - N_EXPERTS
    g_weight = 1.0 / jnp.sum(jnp.exp(gl - g_max), axis=1, keepdims=True)

    in_grp = jnp.right_shift(lane, 2) == g_sel
    el = jnp.where(in_grp, logits, neg_inf)
    v1 = jnp.max(el, axis=1, keepdims=True)
    i1 = jnp.min(jnp.where(el == v1, lane, LANES), axis=1, keepdims=True)
    el2 = jnp.where(lane == i1, neg_inf, el)
    v2 = jnp.max(el2, axis=1, keepdims=True)
    i2 = jnp.min(jnp.where(el2 == v2, lane, LANES), axis=1, keepdims=True)
    e2 = jnp.exp(v2 - v1)
    w1 = g_weight / (1.0 + e2)
    w2 = g_weight * e2 / (1.0 + e2)
    gate_ref[...] = jnp.where(lane == i1, w1, jnp.where(lane == i2, w2, 0.0))


def _output_projection(attn, pool, x2d, w_out, ln_g, ln_b, w_gr, b_gr, w_er, b_er):
    T, D = x2d.shape
    tm = TM_OUT
    bf = jnp.bfloat16
    woa = w_out[:Q_COLS].astype(bf)
    wop = w_out[Q_COLS:].astype(bf)
    w_r = jnp.pad(jnp.concatenate([w_er, w_gr], axis=1), ((0, 0), (0, LANES - N_EXPERTS - N_EXPERT_GROUPS)))
    w_r_hi = w_r.astype(bf)
    w_r_lo = (w_r - w_r_hi.astype(jnp.float32)).astype(bf)
    wr = jnp.concatenate([w_r_hi, w_r_lo], axis=1)
    br = jnp.pad(jnp.concatenate([b_er, b_gr]), (0, LANES - N_EXPERTS - N_EXPERT_GROUPS)).reshape(1, LANES)
    full = lambda a: pl.BlockSpec(a.shape, lambda i: (0,) * a.ndim)
    consts = (woa, wop, ln_g.reshape(1, D), ln_b.reshape(1, D), wr, br)
    return pl.pallas_call(
        _out_kernel,
        grid=(T // tm,),
        in_specs=[pl.BlockSpec((tm, Q_COLS), lambda i: (i, 0)),
                  pl.BlockSpec((tm, D_POOL), lambda i: (i, 0)),
                  pl.BlockSpec((tm, D), lambda i: (i, 0))] + [full(c) for c in consts],
        out_specs=(pl.BlockSpec((tm, D), lambda i: (i, 0)),
                   pl.BlockSpec((tm, D), lambda i: (i, 0)),
                   pl.BlockSpec((tm, LANES), lambda i: (i, 0))),
        out_shape=(jax.ShapeDtypeStruct((T, D), jnp.float32),
                   jax.ShapeDtypeStruct((T, D), bf),
                   jax.ShapeDtypeStruct((T, LANES), jnp.float32)),
        compiler_params=pltpu.CompilerParams(
            dimension_semantics=("arbitrary",), vmem_limit_bytes=VMEM_LIMIT_BYTES),
        name="dsa_output_projection",
    )(attn, pool, x2d, *consts)


def _moe_kernel(x1b_ref, gate_ref, x1_ref, wg_ref, wu_ref, wd_ref, g_ref, b_ref, out_ref, hid_ref):
    xb = x1b_ref[...]
    gate = gate_ref[...]
    lane = lax.broadcasted_iota(jnp.int32, gate.shape, 1)
    for e in range(N_EXPERTS):
        a = _dot(xb, wg_ref[e])
        b = _dot(xb, wu_ref[e])
        gcol = jnp.sum(jnp.where(lane == e, gate, 0.0), axis=1, keepdims=True)
        hid = (a * jax.nn.sigmoid(a)) * b * gcol
        hid_ref[:, e * D_EXPERT:(e + 1) * D_EXPERT] = hid.astype(jnp.bfloat16)
    y = _dot(hid_ref[...], wd_ref[...])
    out_ref[...] = _layer_norm(ALPHA * x1_ref[...] + y, g_ref[...], b_ref[...])


def _experts(x1b, gate, x1, w_gate, w_up, w_down, ln_g, ln_b):
    T, D = x1.shape
    tm = TM_MOE
    bf = jnp.bfloat16
    wg = w_gate.reshape(N_EXPERTS, D, D_EXPERT).astype(bf)
    wu = w_up.reshape(N_EXPERTS, D, D_EXPERT).astype(bf)
    wd = w_down.reshape(N_EXPERTS * D_EXPERT, D).astype(bf)
    resident = lambda a: pl.BlockSpec(a.shape, lambda i: (0,) * a.ndim, pipeline_mode=pl.Buffered(1))
    consts = (wg, wu, wd, ln_g.reshape(1, D), ln_b.reshape(1, D))
    return pl.pallas_call(
        _moe_kernel,
        grid=(T // tm,),
        in_specs=[pl.BlockSpec((tm, D), lambda i: (i, 0)),
                  pl.BlockSpec((tm, LANES), lambda i: (i, 0)),
                  pl.BlockSpec((tm, D), lambda i: (i, 0))] + [resident(c) for c in consts],
        out_specs=pl.BlockSpec((tm, D), lambda i: (i, 0)),
        out_shape=jax.ShapeDtypeStruct((T, D), jnp.float32),
        scratch_shapes=[pltpu.VMEM((tm, N_EXPERTS * D_EXPERT), bf)],
        compiler_params=pltpu.CompilerParams(
            dimension_semantics=("arbitrary",), vmem_limit_bytes=VMEM_LIMIT_BYTES),
        name="dsa_experts",
    )(x1b, gate, x1, *consts)


def kernel(x, w_in, w_pool, pool_scale, w_out, ln1_g, ln1_b, w_group_router, b_group_router,
           w_expert_router, b_expert_router, w_gate, w_up, w_down, ln2_g, ln2_b):
    B, S, D = x.shape
    assert (B, S, D)[1:] == (S, D_MODEL) and S % TM_PROJ == 0 and S % TQ == 0 and TQ == TK
    assert w_in.shape[0] == DEPTH == 1
    l = 0
    qT, vT, qiT, wiT, k0, k1, ki, pool = _input_projection(x, w_in[l], w_pool[l], pool_scale[l])
    attn = _dsa_attention(qT, qiT, wiT, k0, k1, ki, vT)
    x2d = x.reshape(B * S, D)
    x1, x1b, gate = _output_projection(
        attn.reshape(B * S, Q_COLS), pool.reshape(B * S, D_POOL), x2d, w_out[l], ln1_g[l], ln1_b[l],
        w_group_router[l], b_group_router[l], w_expert_router[l], b_expert_router[l])
    out = _experts(x1b, gate, x1, w_gate[l], w_up[l], w_down[l], ln2_g[l], ln2_b[l])
    return out.reshape(B, S, D)
```

```python
import functools
import math

import jax
import jax.numpy as jnp
from jax import lax
from jax.experimental import pallas as pl
from jax.experimental.pallas import tpu as pltpu

D_MODEL = 1024
N_HEADS = 8
HEAD_DIM = 64
N_KV_HEADS = 2
Q_PER_KV = N_HEADS // N_KV_HEADS
IDX_HEADS = 8
IDX_DIM = 64
TOPK_MAX = 256
POOL_WINDOWS = (2, 4, 8, 16)
N_POOL_GROUPS = len(POOL_WINDOWS)
POOL_GROUP_DIM = 128
D_POOL = N_POOL_GROUPS * POOL_GROUP_DIM
MAX_WIN = max(POOL_WINDOWS)
Q_COLS = N_HEADS * HEAD_DIM
KV_COLS = N_KV_HEADS * HEAD_DIM
QI_COLS = IDX_HEADS * IDX_DIM
N_EXPERT_GROUPS = 4
EXPERTS_PER_GROUP = 4
N_EXPERTS = N_EXPERT_GROUPS * EXPERTS_PER_GROUP
D_EXPERT = 256
DEPTH = 1
ALPHA = float((2 * DEPTH) ** 0.25)
LN_EPS = 1e-5

LANES = 128
SUBLANES = 8
MXU_DIM = 256
VMEM_LIMIT_BYTES = 56 * 1024 * 1024

TM_PROJ = 512
TQ = MXU_DIM
TK = MXU_DIM
TKS = LANES
SUB_PER_CHUNK = TK // TKS
TM_OUT = 512
TM_MOE = 512

F32_MAX = float(jnp.finfo(jnp.float32).max)
KEEP_ALL = 1.0e9
BISECT_MAX_ITERS = 40

_NT_DIMS = (((1,), (1,)), ((), ()))


def _dot(a, b):
    return jnp.dot(a, b, preferred_element_type=jnp.float32)


def _dot_nt(a, b):
    return lax.dot_general(a, b, _NT_DIMS, preferred_element_type=jnp.float32)


def _proj_kernel(x_ref, wqT_ref, wvT_ref, wqiT_ref, wwiT_ref, wk0_ref, wk1_ref, wki_ref, wpin_ref,
                 wpool_ref, pscale_ref,
                 qT_ref, vT_ref, qiT_ref, wiT_ref, k_ref, ki_ref, pool_ref,
                 hist_ref, ext_ref, *, idx_scale):
    s_blk = pl.program_id(1)
    tm = x_ref.shape[1]
    xb = x_ref[0].astype(jnp.bfloat16)

    qT_ref[0] = _dot_nt(wqT_ref[...], xb).astype(jnp.bfloat16)
    qiT_ref[0] = _dot_nt(wqiT_ref[...], xb).astype(jnp.bfloat16)
    wiT_ref[0] = _dot_nt(wwiT_ref[...], xb)[:IDX_HEADS] * idx_scale
    vT = _dot_nt(wvT_ref[...], xb).astype(jnp.bfloat16)
    for g in range(N_KV_HEADS):
        for j in range(tm // TKS):
            vT_ref[0, g, j] = vT[g * HEAD_DIM:(g + 1) * HEAD_DIM, j * TKS:(j + 1) * TKS]

    k_ref[0, 0] = _dot(xb, wk0_ref[...]).astype(jnp.bfloat16)
    k_ref[0, 1] = _dot(xb, wk1_ref[...]).astype(jnp.bfloat16)
    ki_ref[0] = _dot(xb, wki_ref[...]).astype(jnp.bfloat16)

    u = _dot(xb, wpin_ref[...])

    @pl.when(s_blk == 0)
    def _():
        hist_ref[...] = jnp.zeros_like(hist_ref)

    ext_ref[0:MAX_WIN, :] = hist_ref[...]
    ext_ref[MAX_WIN:MAX_WIN + tm, :] = u
    hist_ref[...] = u[tm - MAX_WIN:tm, :]

    pos = s_blk * tm + lax.broadcasted_iota(jnp.int32, (tm, POOL_GROUP_DIM), 0)
    for g, win in enumerate(POOL_WINDOWS):
        c0 = g * POOL_GROUP_DIM
        acc = ext_ref[MAX_WIN:MAX_WIN + tm, c0:c0 + POOL_GROUP_DIM]
        for j in range(1, win):
            acc = acc + ext_ref[MAX_WIN - j:MAX_WIN - j + tm, c0:c0 + POOL_GROUP_DIM]
        count = jnp.minimum(pos + 1, win).astype(jnp.float32)
        d = acc / count - ext_ref[MAX_WIN:MAX_WIN + tm, c0:c0 + POOL_GROUP_DIM]
        y = _dot(d.astype(jnp.bfloat16), wpool_ref[g]) * pscale_ref[:, c0:c0 + POOL_GROUP_DIM]
        pool_ref[0, :, c0:c0 + POOL_GROUP_DIM] = y.astype(jnp.bfloat16)


def _input_projection(x, w_in, w_pool, pool_scale):
    B, S, D = x.shape
    tm = TM_PROJ
    bf = jnp.bfloat16
    idx_scale = (IDX_DIM ** -0.5) * (IDX_HEADS ** -0.5)
    q_scale = (HEAD_DIM ** -0.5) * math.log2(math.e)

    o = 0
    w_q = w_in[:, o:o + Q_COLS]; o += Q_COLS
    w_k = w_in[:, o:o + KV_COLS]; o += KV_COLS
    w_v = w_in[:, o:o + KV_COLS]; o += KV_COLS
    w_pin = w_in[:, o:o + D_POOL]; o += D_POOL
    w_qi = w_in[:, o:o + QI_COLS]; o += QI_COLS
    w_ki = w_in[:, o:o + IDX_DIM]; o += IDX_DIM
    w_wi = w_in[:, o:o + IDX_HEADS]

    wqT = (w_q * q_scale).T.astype(bf)
    wvT = w_v.T.astype(bf)
    wqiT = w_qi.T.astype(bf)
    wwiT = jnp.pad(w_wi.T, ((0, 2 * SUBLANES - IDX_HEADS), (0, 0))).astype(bf)
    wk0 = w_k[:, :HEAD_DIM].astype(bf)
    wk1 = w_k[:, HEAD_DIM:].astype(bf)

    full = lambda a: pl.BlockSpec(a.shape, lambda b, s: (0,) * a.ndim)
    weights = (wqT, wvT, wqiT, wwiT, wk0, wk1, w_ki.astype(bf), w_pin.astype(bf),
               w_pool.astype(bf), pool_scale.reshape(1, D_POOL))
    out_shape = (
        jax.ShapeDtypeStruct((B, Q_COLS, S), bf),
        jax.ShapeDtypeStruct((B, N_KV_HEADS, S // TKS, HEAD_DIM, TKS), bf),
        jax.ShapeDtypeStruct((B, QI_COLS, S), bf),
        jax.ShapeDtypeStruct((B, IDX_HEADS, S), jnp.float32),
        jax.ShapeDtypeStruct((B, N_KV_HEADS, S, HEAD_DIM), bf),
        jax.ShapeDtypeStruct((B, S, IDX_DIM), bf),
        jax.ShapeDtypeStruct((B, S, D_POOL), bf),
    )
    out_specs = (
        pl.BlockSpec((1, Q_COLS, tm), lambda b, s: (b, 0, s)),
        pl.BlockSpec((1, N_KV_HEADS, tm // TKS, HEAD_DIM, TKS), lambda b, s: (b, 0, s, 0, 0)),
        pl.BlockSpec((1, QI_COLS, tm), lambda b, s: (b, 0, s)),
        pl.BlockSpec((1, IDX_HEADS, tm), lambda b, s: (b, 0, s)),
        pl.BlockSpec((1, N_KV_HEADS, tm, HEAD_DIM), lambda b, s: (b, 0, s, 0)),
        pl.BlockSpec((1, tm, IDX_DIM), lambda b, s: (b, s, 0)),
        pl.BlockSpec((1, tm, D_POOL), lambda b, s: (b, s, 0)),
    )
    return pl.pallas_call(
        functools.partial(_proj_kernel, idx_scale=idx_scale),
        grid=(B, S // tm),
        in_specs=[pl.BlockSpec((1, tm, D), lambda b, s: (b, s, 0))] + [full(w) for w in weights],
        out_specs=out_specs,
        out_shape=out_shape,
        scratch_shapes=[pltpu.VMEM((MAX_WIN, D_POOL), jnp.float32),
                        pltpu.VMEM((MAX_WIN + tm, D_POOL), jnp.float32)],
        compiler_params=pltpu.CompilerParams(
            dimension_semantics=("arbitrary", "arbitrary"), vmem_limit_bytes=VMEM_LIMIT_BYTES),
        name="dsa_input_projection",
    )(x, *weights)


def _dsa_kernel(qT_ref, qiT_ref, wiT_ref, k_ref, ki_ref, vT_ref, out_ref,
                scr_ref, s_ref, m_ref, l_ref, acc_ref):
    q_blk = pl.program_id(1)
    n_chunks = q_blk + 1
    q0 = q_blk * TQ
    topk = float(TOPK_MAX)
    neg_inf = -jnp.inf

    row = lambda v: jnp.full((1, TQ), v, jnp.float32)
    flag = lambda cond: jnp.where(cond, 1.0, 0.0)
    qpos = q0 + lax.broadcasted_iota(jnp.int32, (1, TQ), 1)
    kiota = lax.broadcasted_iota(jnp.int32, (TK, TQ), 0)

    def load_chunk(c):
        return scr_ref[pl.ds(c * SUB_PER_CHUNK, SUB_PER_CHUNK)].reshape(TK, TQ)

    def store_chunk(c, val):
        scr_ref[pl.ds(c * SUB_PER_CHUNK, SUB_PER_CHUNK)] = val.reshape(SUB_PER_CHUNK, TKS, TQ)

    def fold(x):
        return jnp.sum(x.reshape(TK // SUBLANES, SUBLANES, TQ), axis=0)

    def chunk_scores(c):
        kic = ki_ref[0, c]
        sc = jnp.zeros((TK, TQ), jnp.float32)
        for h in range(IDX_HEADS):
            z = _dot(kic, qiT_ref[0, h * IDX_DIM:(h + 1) * IDX_DIM, :])
            sc = sc + jnp.maximum(z, 0.0) * wiT_ref[0, h:h + 1, :]
        return sc

    def full_chunk(c, carry):
        mn, mx = carry
        sc = chunk_scores(c)
        store_chunk(c, sc)
        mx = jnp.maximum(mx, jnp.max(sc, axis=0, keepdims=True))
        mn = jnp.minimum(mn, jnp.min(sc, axis=0, keepdims=True))
        return mn, mx

    smin, smax = lax.fori_loop(0, q_blk, full_chunk, (row(jnp.inf), row(neg_inf)))
    sc = chunk_scores(q_blk)
    valid = (q0 + kiota) <= qpos
    store_chunk(q_blk, jnp.where(valid, sc, neg_inf))
    smax = jnp.maximum(smax, jnp.max(jnp.where(valid, sc, neg_inf), axis=0, keepdims=True))
    smin = jnp.minimum(smin, jnp.min(jnp.where(valid, sc, jnp.inf), axis=0, keepdims=True))

    def count_ge(p):
        def body(c, acc):
            return acc + fold(flag(load_chunk(c) >= p))
        acc = lax.fori_loop(0, n_chunks, body, jnp.zeros((SUBLANES, TQ), jnp.float32))
        return jnp.sum(acc, axis=0, keepdims=True)

    n_valid = (qpos + 1).astype(jnp.float32)
    short = flag(n_valid <= topk)

    def zero_probe(c, carry):
        a_ge, a_gt = carry
        blk = load_chunk(c)
        return a_ge + fold(flag(blk >= 0.0)), a_gt + fold(flag(blk > 0.0))
    z8 = jnp.zeros((SUBLANES, TQ), jnp.float32)
    a_ge, a_gt = lax.fori_loop(0, n_chunks, zero_probe, (z8, z8))
    c_ge0 = jnp.sum(a_ge, axis=0, keepdims=True)
    c_gt0 = jnp.sum(a_gt, axis=0, keepdims=True)
    at_zero = (1.0 - short) * flag(c_ge0 >= topk) * flag(c_gt0 <= topk)
    above = (1.0 - short) * flag(c_gt0 > topk)
    below = (1.0 - short) * flag(c_ge0 < topk)

    thr = jnp.where(short > 0.0, -F32_MAX, 0.0)
    keep = jnp.where(at_zero * flag(c_ge0 > topk) > 0.0, topk - c_gt0, KEEP_ALL)
    done = jnp.maximum(short, at_zero)
    lo = jnp.where(above > 0.0, 0.0, smin)
    c_lo = jnp.where(above > 0.0, c_ge0, n_valid)
    hi = jnp.where(below > 0.0, 0.0, jnp.inf)
    c_hi = jnp.where(below > 0.0, c_ge0, 0.0)

    def n_open(done, stuck):
        return jnp.sum(flag(done + stuck <= 0.0))

    def bisect_cond(st):
        return jnp.logical_and(st[0] < BISECT_MAX_ITERS, st[1] > 0.0)

    def bisect_body(st):
        it, _, lo, hi, c_lo, c_hi, thr, done, stuck = st
        p = 0.5 * lo + 0.5 * jnp.minimum(hi, smax)
        stuck = jnp.maximum(stuck, 1.0 - flag(p > lo) * flag(p < hi))
        active = flag(done + stuck <= 0.0)
        c = count_ge(p)
        is_eq = active * flag(c == topk)
        is_gt = active * flag(c > topk)
        is_lt = active * flag(c < topk)
        thr = jnp.where(is_eq > 0.0, p, thr)
        done = jnp.maximum(done, is_eq)
        lo = jnp.where(is_gt > 0.0, p, lo)
        c_lo = jnp.where(is_gt > 0.0, c, c_lo)
        hi = jnp.where(is_lt > 0.0, p, hi)
        c_hi = jnp.where(is_lt > 0.0, c, c_hi)
        return (it + 1, n_open(done, stuck), lo, hi, c_lo, c_hi, thr, done, stuck)

    st = (jnp.int32(0), n_open(done, row(0.0)), lo, hi, c_lo, c_hi, thr, done, row(0.0))
    st = lax.while_loop(bisect_cond, bisect_body, st)
    _, _, _, hi, _, c_hi, thr, done, _ = st

    def walk_cond(st):
        return st[0] > 0.0

    def walk_body(st):
        _, hi, c_hi, thr, keep, done = st
        def below_max(c, acc):
            blk = load_chunk(c)
            part = jnp.where(blk < hi, blk, neg_inf)
            return jnp.maximum(acc, jnp.max(part.reshape(TK // SUBLANES, SUBLANES, TQ), axis=0))
        m8 = lax.fori_loop(0, n_chunks, below_max, jnp.full((SUBLANES, TQ), neg_inf, jnp.float32))
        m = jnp.max(m8, axis=0, keepdims=True)
        c = count_ge(m)
        active = 1.0 - done
        reach = active * flag(c >= topk)
        thr = jnp.where(reach > 0.0, m, thr)
        keep = jnp.where(reach > 0.0, topk - c_hi, keep)
        done = jnp.maximum(done, reach)
        step = active * (1.0 - reach)
        hi = jnp.where(step > 0.0, m, hi)
        c_hi = jnp.where(step > 0.0, c, c_hi)
        return (jnp.sum(1.0 - done), hi, c_hi, thr, keep, done)

    st = (jnp.sum(1.0 - done), hi, c_hi, thr, keep, done)
    _, _, _, thr, keep, _ = lax.while_loop(walk_cond, walk_body, st)

    has_ties = jnp.sum(flag(keep < KEEP_ALL)) > 0.0

    @pl.when(jnp.logical_not(has_ties))
    def _():
        def body(c, carry):
            store_chunk(c, jnp.where(load_chunk(c) >= thr, 0.0, neg_inf))
            return carry
        lax.fori_loop(0, n_chunks, body, 0)

    @pl.when(has_ties)
    def _():
        tri = flag(lax.broadcasted_iota(jnp.int32, (TK, TK), 1)
                   <= lax.broadcasted_iota(jnp.int32, (TK, TK), 0)).astype(jnp.bfloat16)
        def body(c, seen):
            blk = load_chunk(c)
            eq = flag(blk == thr)
            rank = seen + _dot(tri, eq.astype(jnp.bfloat16))
            sel = jnp.maximum(flag(blk > thr), eq * flag(rank <= keep))
            store_chunk(c, jnp.where(sel > 0.0, 0.0, neg_inf))
            return seen + jnp.sum(fold(eq), axis=0, keepdims=True)
        lax.fori_loop(0, n_chunks, body, row(0.0))

    m_ref[...] = jnp.full_like(m_ref, neg_inf)
    l_ref[...] = jnp.zeros_like(l_ref)
    acc_ref[...] = jnp.zeros_like(acc_ref)

    def attend(c, carry):
        for u in range(SUB_PER_CHUNK):
            j = c * SUB_PER_CHUNK + u
            for h in range(N_HEADS):
                s_ref[u, h] = _dot(k_ref[0, h // Q_PER_KV, j],
                                   qT_ref[0, h * HEAD_DIM:(h + 1) * HEAD_DIM, :])
        for u in range(SUB_PER_CHUNK):
            j = c * SUB_PER_CHUNK + u
            bias = scr_ref[j]
            for h in range(N_HEADS):
                g = h // Q_PER_KV
                s = s_ref[u, h] + bias
                m_old = m_ref[h]
                m_new = jnp.maximum(m_old, jnp.max(s, axis=0, keepdims=True))
                m_safe = jnp.where(m_new == neg_inf, 0.0, m_new)
                corr = jnp.exp2(m_old - m_safe)
                p = jnp.exp2(s - m_safe)
                l_ref[h] = l_ref[h] * corr + jnp.sum(p, axis=0, keepdims=True)
                acc_ref[h] = acc_ref[h] * corr + _dot(vT_ref[0, g, j], p.astype(jnp.bfloat16))
                m_ref[h] = m_new
        return carry

    lax.fori_loop(0, n_chunks, attend, 0)
    for h in range(N_HEADS):
        acc_ref[h] = acc_ref[h] / l_ref[h]

    outT = acc_ref[...].reshape(N_HEADS * HEAD_DIM, TQ)
    out_ref[0] = outT.T.astype(jnp.bfloat16)


def _dsa_attention(qT, qiT, wiT, k, ki, vT):
    B, _, S = qT.shape
    k = k.reshape(B, N_KV_HEADS, S // TKS, TKS, HEAD_DIM)
    ki = ki.reshape(B, S // TK, TK, IDX_DIM)
    per_batch = lambda a: pl.BlockSpec((1,) + a.shape[1:], lambda b, i: (b,) + (0,) * (a.ndim - 1))
    return pl.pallas_call(
        _dsa_kernel,
        grid=(B, S // TQ),
        in_specs=[
            pl.BlockSpec((1, Q_COLS, TQ), lambda b, i: (b, 0, i)),
            pl.BlockSpec((1, QI_COLS, TQ), lambda b, i: (b, 0, i)),
            pl.BlockSpec((1, IDX_HEADS, TQ), lambda b, i: (b, 0, i)),
            per_batch(k), per_batch(ki), per_batch(vT),
        ],
        out_specs=pl.BlockSpec((1, TQ, Q_COLS), lambda b, i: (b, i, 0)),
        out_shape=jax.ShapeDtypeStruct((B, S, Q_COLS), jnp.bfloat16),
        scratch_shapes=[
            pltpu.VMEM((S // TKS, TKS, TQ), jnp.float32),
            pltpu.VMEM((SUB_PER_CHUNK, N_HEADS, TKS, TQ), jnp.float32),
            pltpu.VMEM((N_HEADS, 1, TQ), jnp.float32),
            pltpu.VMEM((N_HEADS, 1, TQ), jnp.float32),
            pltpu.VMEM((N_HEADS, HEAD_DIM, TQ), jnp.float32),
        ],
        compiler_params=pltpu.CompilerParams(
            dimension_semantics=("arbitrary", "arbitrary"), vmem_limit_bytes=VMEM_LIMIT_BYTES),
        name="dsa_attention",
    )(qT, qiT, wiT, k, ki, vT)


def _layer_norm(z, g, b):
    mu = jnp.mean(z, axis=-1, keepdims=True)
    zc = z - mu
    var = jnp.mean(zc * zc, axis=-1, keepdims=True)
    return zc * lax.rsqrt(var + LN_EPS) * g + b


def _out_kernel(attn_ref, pool_ref, x_ref, woa_ref, wop_ref, g_ref, b_ref, wr_ref, br_ref,
                x1_ref, x1b_ref, gate_ref):
    mix = _dot(attn_ref[...], woa_ref[...]) + _dot(pool_ref[...], wop_ref[...])
    x1 = _layer_norm(ALPHA * x_ref[...] + mix, g_ref[...], b_ref[...])
    x1_ref[...] = x1
    hi = x1.astype(jnp.bfloat16)
    x1b_ref[...] = hi
    lo = (x1 - hi.astype(jnp.float32)).astype(jnp.bfloat16)

    r = _dot(hi, wr_ref[...])
    logits = r[:, :LANES] + r[:, LANES:] + _dot(lo, wr_ref[:, :LANES]) + br_ref[...]

    tm = logits.shape[0]
    lane = lax.broadcasted_iota(jnp.int32, (tm, LANES), 1)
    neg_inf = -jnp.inf
    is_grp = jnp.where(lane >= N_EXPERTS, jnp.where(lane < N_EXPERTS + N_EXPERT_GROUPS, 1.0, 0.0), 0.0)
    gl = jnp.where(is_grp > 0.0, logits, neg_inf)
    g_max = jnp.max(gl, axis=1, keepdims=True)
    g_sel = jnp.min(jnp.where(gl == g_max, lane, LANES), axis=1, keepdims=True) - N_EXPERTS
    g_weight = 1.0 / jnp.sum(jnp.exp(gl - g_max), axis=1, keepdims=True)

    in_grp = jnp.right_shift(lane, 2) == g_sel
    el = jnp.where(in_grp, logits, neg_inf)
    v1 = jnp.max(el, axis=1, keepdims=True)
    i1 = jnp.min(jnp.where(el == v1, lane, LANES), axis=1, keepdims=True)
    el2 = jnp.where(lane == i1, neg_inf, el)
    v2 = jnp.max(el2, axis=1, keepdims=True)
    i2 = jnp.min(jnp.where(el2 == v2, lane, LANES), axis=1, keepdims=True)
    e2 = jnp.exp(v2 - v1)
    w1 = g_weight / (1.0 + e2)
    w2 = g_weight * e2 / (1.0 + e2)
    gate_ref[...] = jnp.where(lane == i1, w1, jnp.where(lane == i2, w2, 0.0))


def _output_projection(attn, pool, x2d, w_out, ln_g, ln_b, w_gr, b_gr, w_er, b_er):
    T, D = x2d.shape
    tm = TM_OUT
    bf = jnp.bfloat16
    woa = w_out[:Q_COLS].astype(bf)
    wop = w_out[Q_COLS:].astype(bf)
    w_r = jnp.pad(jnp.concatenate([w_er, w_gr], axis=1), ((0, 0), (0, LANES - N_EXPERTS - N_EXPERT_GROUPS)))
    w_r_hi = w_r.astype(bf)
    w_r_lo = (w_r - w_r_hi.astype(jnp.float32)).astype(bf)
    wr = jnp.concatenate([w_r_hi, w_r_lo], axis=1)
    br = jnp.pad(jnp.concatenate([b_er, b_gr]), (0, LANES - N_EXPERTS - N_EXPERT_GROUPS)).reshape(1, LANES)
    full = lambda a: pl.BlockSpec(a.shape, lambda i: (0,) * a.ndim)
    consts = (woa, wop, ln_g.reshape(1, D), ln_b.reshape(1, D), wr, br)
    return pl.pallas_call(
        _out_kernel,
        grid=(T // tm,),
        in_specs=[pl.BlockSpec((tm, Q_COLS), lambda i: (i, 0)),
                  pl.BlockSpec((tm, D_POOL), lambda i: (i, 0)),
                  pl.BlockSpec((tm, D), lambda i: (i, 0))] + [full(c) for c in consts],
        out_specs=(pl.BlockSpec((tm, D), lambda i: (i, 0)),
                   pl.BlockSpec((tm, D), lambda i: (i, 0)),
                   pl.BlockSpec((tm, LANES), lambda i: (i, 0))),
        out_shape=(jax.ShapeDtypeStruct((T, D), jnp.float32),
                   jax.ShapeDtypeStruct((T, D), bf),
                   jax.ShapeDtypeStruct((T, LANES), jnp.float32)),
        compiler_params=pltpu.CompilerParams(
            dimension_semantics=("arbitrary",), vmem_limit_bytes=VMEM_LIMIT_BYTES),
        name="dsa_output_projection",
    )(attn, pool, x2d, *consts)


def _moe_kernel(x1b_ref, gate_ref, x1_ref, wg_ref, wu_ref, wd_ref, g_ref, b_ref, out_ref, hid_ref):
    xb = x1b_ref[...]
    gate = gate_ref[...]
    lane = lax.broadcasted_iota(jnp.int32, gate.shape, 1)
    for e in range(N_EXPERTS):
        a = _dot(xb, wg_ref[e])
        b = _dot(xb, wu_ref[e])
        gcol = jnp.sum(jnp.where(lane == e, gate, 0.0), axis=1, keepdims=True)
        hid = (a * jax.nn.sigmoid(a)) * b * gcol
        hid_ref[:, e * D_EXPERT:(e + 1) * D_EXPERT] = hid.astype(jnp.bfloat16)
    y = _dot(hid_ref[...], wd_ref[...])
    out_ref[...] = _layer_norm(ALPHA * x1_ref[...] + y, g_ref[...], b_ref[...])


def _experts(x1b, gate, x1, w_gate, w_up, w_down, ln_g, ln_b):
    T, D = x1.shape
    tm = TM_MOE
    bf = jnp.bfloat16
    wg = w_gate.reshape(N_EXPERTS, D, D_EXPERT).astype(bf)
    wu = w_up.reshape(N_EXPERTS, D, D_EXPERT).astype(bf)
    wd = w_down.reshape(N_EXPERTS * D_EXPERT, D).astype(bf)
    resident = lambda a: pl.BlockSpec(a.shape, lambda i: (0,) * a.ndim, pipeline_mode=pl.Buffered(1))
    consts = (wg, wu, wd, ln_g.reshape(1, D), ln_b.reshape(1, D))
    return pl.pallas_call(
        _moe_kernel,
        grid=(T // tm,),
        in_specs=[pl.BlockSpec((tm, D), lambda i: (i, 0)),
                  pl.BlockSpec((tm, LANES), lambda i: (i, 0)),
                  pl.BlockSpec((tm, D), lambda i: (i, 0))] + [resident(c) for c in consts],
        out_specs=pl.BlockSpec((tm, D), lambda i: (i, 0)),
        out_shape=jax.ShapeDtypeStruct((T, D), jnp.float32),
        scratch_shapes=[pltpu.VMEM((tm, N_EXPERTS * D_EXPERT), bf)],
        compiler_params=pltpu.CompilerParams(
            dimension_semantics=("arbitrary",), vmem_limit_bytes=VMEM_LIMIT_BYTES),
        name="dsa_experts",
    )(x1b, gate, x1, *consts)


def kernel(x, w_in, w_pool, pool_scale, w_out, ln1_g, ln1_b, w_group_router, b_group_router,
           w_expert_router, b_expert_router, w_gate, w_up, w_down, ln2_g, ln2_b):
    B, S, D = x.shape
    assert D == D_MODEL and S % TM_PROJ == 0 and S % TQ == 0 and TQ == TK
    assert w_in.shape[0] == DEPTH == 1
    l = 0
    qT, vT, qiT, wiT, k, ki, pool = _input_projection(x, w_in[l], w_pool[l], pool_scale[l])
    attn = _dsa_attention(qT, qiT, wiT, k, ki, vT)
    x2d = x.reshape(B * S, D)
    x1, x1b, gate = _output_projection(
        attn.reshape(B * S, Q_COLS), pool.reshape(B * S, D_POOL), x2d, w_out[l], ln1_g[l], ln1_b[l],
        w_group_router[l], b_group_router[l], w_expert_router[l], b_expert_router[l])
    out = _experts(x1b, gate, x1, w_gate[l], w_up[l], w_down[l], ln2_g[l], ln2_b[l])
    return out.reshape(B, S, D)
```

```python
import functools
import math

import jax
import jax.numpy as jnp
from jax import lax
from jax.experimental import pallas as pl
from jax.experimental.pallas import tpu as pltpu

D_MODEL = 1024
N_HEADS = 8
HEAD_DIM = 64
N_KV_HEADS = 2
Q_PER_KV = N_HEADS // N_KV_HEADS
IDX_HEADS = 8
IDX_DIM = 64
TOPK_MAX = 256
POOL_WINDOWS = (2, 4, 8, 16)
N_POOL_GROUPS = len(POOL_WINDOWS)
POOL_GROUP_DIM = 128
D_POOL = N_POOL_GROUPS * POOL_GROUP_DIM
MAX_WIN = max(POOL_WINDOWS)
Q_COLS = N_HEADS * HEAD_DIM
KV_COLS = N_KV_HEADS * HEAD_DIM
QI_COLS = IDX_HEADS * IDX_DIM
N_EXPERT_GROUPS = 4
EXPERTS_PER_GROUP = 4
N_EXPERTS = N_EXPERT_GROUPS * EXPERTS_PER_GROUP
D_EXPERT = 256
DEPTH = 1
ALPHA = float((2 * DEPTH) ** 0.25)
LN_EPS = 1e-5

LANES = 128
SUBLANES = 8
MXU_DIM = 256
VMEM_LIMIT_BYTES = 56 * 1024 * 1024

TM_PROJ = 512
TQ = MXU_DIM
TK = MXU_DIM
TM_TAIL = 512

F32_MAX = float(jnp.finfo(jnp.float32).max)
KEEP_ALL = 1.0e9
BISECT_MAX_ITERS = 40

_NT_DIMS = (((1,), (1,)), ((), ()))


def _dot(a, b):
    return jnp.dot(a, b, preferred_element_type=jnp.float32)


def _dot_nt(a, b):
    return lax.dot_general(a, b, _NT_DIMS, preferred_element_type=jnp.float32)


def _proj_kernel(x_ref, wqT_ref, wvT_ref, wqiT_ref, wwiT_ref, wk0_ref, wk1_ref, wki_ref, wpin_ref,
                 wpool_ref, pscale_ref,
                 qT_ref, vT_ref, qiT_ref, wiT_ref, k_ref, ki_ref, pool_ref,
                 hist_ref, ext_ref, *, idx_scale):
    s_blk = pl.program_id(1)
    tm = x_ref.shape[1]
    xb = x_ref[0].astype(jnp.bfloat16)

    qT_ref[0] = _dot_nt(wqT_ref[...], xb).astype(jnp.bfloat16)
    qiT_ref[0] = _dot_nt(wqiT_ref[...], xb).astype(jnp.bfloat16)
    wiT_ref[0] = _dot_nt(wwiT_ref[...], xb)[:IDX_HEADS] * idx_scale
    vT = _dot_nt(wvT_ref[...], xb).astype(jnp.bfloat16)
    for g in range(N_KV_HEADS):
        for j in range(tm // TK):
            vT_ref[0, g, j] = vT[g * HEAD_DIM:(g + 1) * HEAD_DIM, j * TK:(j + 1) * TK]

    k_ref[0, 0] = _dot(xb, wk0_ref[...]).astype(jnp.bfloat16)
    k_ref[0, 1] = _dot(xb, wk1_ref[...]).astype(jnp.bfloat16)
    ki_ref[0] = _dot(xb, wki_ref[...]).astype(jnp.bfloat16)

    u = _dot(xb, wpin_ref[...])

    @pl.when(s_blk == 0)
    def _():
        hist_ref[...] = jnp.zeros_like(hist_ref)

    ext_ref[0:MAX_WIN, :] = hist_ref[...]
    ext_ref[MAX_WIN:MAX_WIN + tm, :] = u
    hist_ref[...] = u[tm - MAX_WIN:tm, :]

    pos = s_blk * tm + lax.broadcasted_iota(jnp.int32, (tm, POOL_GROUP_DIM), 0)
    for g, win in enumerate(POOL_WINDOWS):
        c0 = g * POOL_GROUP_DIM
        acc = ext_ref[MAX_WIN:MAX_WIN + tm, c0:c0 + POOL_GROUP_DIM]
        for j in range(1, win):
            acc = acc + ext_ref[MAX_WIN - j:MAX_WIN - j + tm, c0:c0 + POOL_GROUP_DIM]
        count = jnp.minimum(pos + 1, win).astype(jnp.float32)
        d = acc / count - ext_ref[MAX_WIN:MAX_WIN + tm, c0:c0 + POOL_GROUP_DIM]
        y = _dot(d.astype(jnp.bfloat16), wpool_ref[g]) * pscale_ref[:, c0:c0 + POOL_GROUP_DIM]
        pool_ref[0, :, c0:c0 + POOL_GROUP_DIM] = y.astype(jnp.bfloat16)


def _input_projection(x, w_in, w_pool, pool_scale):
    B, S, D = x.shape
    tm = TM_PROJ
    bf = jnp.bfloat16
    idx_scale = (IDX_DIM ** -0.5) * (IDX_HEADS ** -0.5)
    q_scale = (HEAD_DIM ** -0.5) * math.log2(math.e)

    o = 0
    w_q = w_in[:, o:o + Q_COLS]; o += Q_COLS
    w_k = w_in[:, o:o + KV_COLS]; o += KV_COLS
    w_v = w_in[:, o:o + KV_COLS]; o += KV_COLS
    w_pin = w_in[:, o:o + D_POOL]; o += D_POOL
    w_qi = w_in[:, o:o + QI_COLS]; o += QI_COLS
    w_ki = w_in[:, o:o + IDX_DIM]; o += IDX_DIM
    w_wi = w_in[:, o:o + IDX_HEADS]

    wqT = (w_q * q_scale).T.astype(bf)
    wvT = w_v.T.astype(bf)
    wqiT = w_qi.T.astype(bf)
    wwiT = jnp.pad(w_wi.T, ((0, 2 * SUBLANES - IDX_HEADS), (0, 0))).astype(bf)
    wk0 = w_k[:, :HEAD_DIM].astype(bf)
    wk1 = w_k[:, HEAD_DIM:].astype(bf)

    full = lambda a: pl.BlockSpec(a.shape, lambda b, s: (0,) * a.ndim)
    weights = (wqT, wvT, wqiT, wwiT, wk0, wk1, w_ki.astype(bf), w_pin.astype(bf),
               w_pool.astype(bf), pool_scale.reshape(1, D_POOL))
    out_shape = (
        jax.ShapeDtypeStruct((B, Q_COLS, S), bf),
        jax.ShapeDtypeStruct((B, N_KV_HEADS, S // TK, HEAD_DIM, TK), bf),
        jax.ShapeDtypeStruct((B, QI_COLS, S), bf),
        jax.ShapeDtypeStruct((B, IDX_HEADS, S), jnp.float32),
        jax.ShapeDtypeStruct((B, N_KV_HEADS, S, HEAD_DIM), bf),
        jax.ShapeDtypeStruct((B, S, IDX_DIM), bf),
        jax.ShapeDtypeStruct((B, S, D_POOL), bf),
    )
    out_specs = (
        pl.BlockSpec((1, Q_COLS, tm), lambda b, s: (b, 0, s)),
        pl.BlockSpec((1, N_KV_HEADS, tm // TK, HEAD_DIM, TK), lambda b, s: (b, 0, s, 0, 0)),
        pl.BlockSpec((1, QI_COLS, tm), lambda b, s: (b, 0, s)),
        pl.BlockSpec((1, IDX_HEADS, tm), lambda b, s: (b, 0, s)),
        pl.BlockSpec((1, N_KV_HEADS, tm, HEAD_DIM), lambda b, s: (b, 0, s, 0)),
        pl.BlockSpec((1, tm, IDX_DIM), lambda b, s: (b, s, 0)),
        pl.BlockSpec((1, tm, D_POOL), lambda b, s: (b, s, 0)),
    )
    return pl.pallas_call(
        functools.partial(_proj_kernel, idx_scale=idx_scale),
        grid=(B, S // tm),
        in_specs=[pl.BlockSpec((1, tm, D), lambda b, s: (b, s, 0))] + [full(w) for w in weights],
        out_specs=out_specs,
        out_shape=out_shape,
        scratch_shapes=[pltpu.VMEM((MAX_WIN, D_POOL), jnp.float32),
                        pltpu.VMEM((MAX_WIN + tm, D_POOL), jnp.float32)],
        compiler_params=pltpu.CompilerParams(
            dimension_semantics=("arbitrary", "arbitrary"), vmem_limit_bytes=VMEM_LIMIT_BYTES),
        name="dsa_input_projection",
    )(x, *weights)


def _dsa_kernel(qT_ref, qiT_ref, wiT_ref, k_ref, ki_ref, vT_ref, out_ref,
                scr_ref, sa_ref, sb_ref, m_ref, l_ref, acc_ref):
    q_blk = pl.program_id(1)
    n_chunks = q_blk + 1
    q0 = q_blk * TQ
    topk = float(TOPK_MAX)
    neg_inf = -jnp.inf

    row = lambda v: jnp.full((1, TQ), v, jnp.float32)
    flag = lambda cond: jnp.where(cond, 1.0, 0.0)
    qpos = q0 + lax.broadcasted_iota(jnp.int32, (1, TQ), 1)
    kiota = lax.broadcasted_iota(jnp.int32, (TK, TQ), 0)

    def load_chunk(c):
        return scr_ref[c]

    def store_chunk(c, val):
        scr_ref[c] = val

    N_ACC = 4

    def fold(x):
        parts = x.reshape(N_ACC, TK // (N_ACC * SUBLANES), SUBLANES, TQ)
        return tuple(jnp.sum(parts[i], axis=0) for i in range(N_ACC))

    def add_folded(accs, x):
        return tuple(a + f for a, f in zip(accs, fold(x)))

    def finish(accs):
        return jnp.sum(functools.reduce(lambda a, b: a + b, accs), axis=0, keepdims=True)

    zero_accs = tuple(jnp.zeros((SUBLANES, TQ), jnp.float32) for _ in range(N_ACC))

    def chunk_scores(c):
        kic = ki_ref[0, c]
        sc = jnp.zeros((TK, TQ), jnp.float32)
        for h in range(IDX_HEADS):
            z = _dot(kic, qiT_ref[0, h * IDX_DIM:(h + 1) * IDX_DIM, :])
            sc = sc + jnp.maximum(z, 0.0) * wiT_ref[0, h:h + 1, :]
        return sc

    def full_chunk(c, carry):
        mn, mx, a_ge, a_gt = carry
        sc = chunk_scores(c)
        store_chunk(c, sc)
        mx = jnp.maximum(mx, jnp.max(sc, axis=0, keepdims=True))
        mn = jnp.minimum(mn, jnp.min(sc, axis=0, keepdims=True))
        return mn, mx, add_folded(a_ge, flag(sc >= 0.0)), add_folded(a_gt, flag(sc > 0.0))

    smin, smax, a_ge, a_gt = lax.fori_loop(
        0, q_blk, full_chunk, (row(jnp.inf), row(neg_inf), zero_accs, zero_accs))
    sc = chunk_scores(q_blk)
    valid = (q0 + kiota) <= qpos
    sc_lo = jnp.where(valid, sc, neg_inf)
    store_chunk(q_blk, sc_lo)
    smax = jnp.maximum(smax, jnp.max(sc_lo, axis=0, keepdims=True))
    smin = jnp.minimum(smin, jnp.min(jnp.where(valid, sc, jnp.inf), axis=0, keepdims=True))
    c_ge0 = finish(add_folded(a_ge, flag(sc_lo >= 0.0)))
    c_gt0 = finish(add_folded(a_gt, flag(sc_lo > 0.0)))

    def count_ge(p):
        def body(c, accs):
            return add_folded(accs, flag(load_chunk(c) >= p))
        return finish(lax.fori_loop(0, n_chunks, body, zero_accs))

    n_valid = (qpos + 1).astype(jnp.float32)
    short = flag(n_valid <= topk)
    at_zero = (1.0 - short) * flag(c_ge0 >= topk) * flag(c_gt0 <= topk)
    above = (1.0 - short) * flag(c_gt0 > topk)
    below = (1.0 - short) * flag(c_ge0 < topk)

    thr = jnp.where(short > 0.0, -F32_MAX, 0.0)
    keep = jnp.where(at_zero * flag(c_ge0 > topk) > 0.0, topk - c_gt0, KEEP_ALL)
    done = jnp.maximum(short, at_zero)
    lo = jnp.where(above > 0.0, 0.0, smin)
    hi = jnp.where(below > 0.0, 0.0, jnp.inf)
    c_hi = jnp.where(below > 0.0, c_ge0, 0.0)

    def n_open(done, parked):
        return jnp.sum(flag(done + parked <= 0.0))

    def bisect_step(lo, hi, c_hi, thr, done, parked):
        p = 0.5 * lo + 0.5 * jnp.minimum(hi, smax)
        parked = jnp.maximum(parked, 1.0 - flag(p > lo) * flag(p < hi))
        parked = jnp.maximum(parked, flag(c_hi == topk - 1.0))
        active = flag(done + parked <= 0.0)
        c = count_ge(p)
        is_eq = active * flag(c == topk)
        is_gt = active * flag(c > topk)
        is_lt = active * flag(c < topk)
        thr = jnp.where(is_eq > 0.0, p, thr)
        done = jnp.maximum(done, is_eq)
        lo = jnp.where(is_gt > 0.0, p, lo)
        hi = jnp.where(is_lt > 0.0, p, hi)
        c_hi = jnp.where(is_lt > 0.0, c, c_hi)
        return lo, hi, c_hi, thr, done, parked

    def bisect_cond(st):
        return jnp.logical_and(st[0] < BISECT_MAX_ITERS, st[1] > 0.0)

    def bisect_body(st):
        vec = bisect_step(*bisect_step(*st[2:]))
        return (st[0] + 2, n_open(vec[4], vec[5])) + vec

    st = (jnp.int32(0), n_open(done, row(0.0)), lo, hi, c_hi, thr, done, row(0.0))
    _, _, _, hi, c_hi, thr, done, _ = lax.while_loop(bisect_cond, bisect_body, st)

    def walk_cond(st):
        return st[0] > 0.0

    def walk_body(st):
        _, hi, c_hi, thr, keep, done = st
        def below_max(c, acc):
            blk = load_chunk(c)
            part = jnp.where(blk < hi, blk, neg_inf)
            return jnp.maximum(acc, jnp.max(part.reshape(TK // SUBLANES, SUBLANES, TQ), axis=0))
        m8 = lax.fori_loop(0, n_chunks, below_max, jnp.full((SUBLANES, TQ), neg_inf, jnp.float32))
        m = jnp.max(m8, axis=0, keepdims=True)
        c = count_ge(m)
        active = 1.0 - done
        reach = active * flag(c >= topk)
        thr = jnp.where(reach > 0.0, m, thr)
        keep = jnp.where(reach * flag(c > topk) > 0.0, topk - c_hi, keep)
        done = jnp.maximum(done, reach)
        step = active * (1.0 - reach)
        hi = jnp.where(step > 0.0, m, hi)
        c_hi = jnp.where(step > 0.0, c, c_hi)
        return (jnp.sum(1.0 - done), hi, c_hi, thr, keep, done)

    st = (jnp.sum(1.0 - done), hi, c_hi, thr, keep, done)
    _, _, _, thr, keep, _ = lax.while_loop(walk_cond, walk_body, st)

    has_ties = jnp.sum(flag(keep < KEEP_ALL)) > 0.0

    @pl.when(jnp.logical_not(has_ties))
    def _():
        def body(c, carry):
            store_chunk(c, jnp.where(load_chunk(c) >= thr, 0.0, neg_inf))
            return carry
        lax.fori_loop(0, n_chunks, body, 0)

    @pl.when(has_ties)
    def _():
        tri = flag(lax.broadcasted_iota(jnp.int32, (TK, TK), 1)
                   <= lax.broadcasted_iota(jnp.int32, (TK, TK), 0)).astype(jnp.bfloat16)
        def body(c, seen):
            blk = load_chunk(c)
            eq = flag(blk == thr)
            rank = seen + _dot(tri, eq.astype(jnp.bfloat16))
            sel = jnp.maximum(flag(blk > thr), eq * flag(rank <= keep))
            store_chunk(c, jnp.where(sel > 0.0, 0.0, neg_inf))
            return seen + finish(fold(eq))
        lax.fori_loop(0, n_chunks, body, row(0.0))

    m_ref[...] = jnp.full_like(m_ref, neg_inf)
    l_ref[...] = jnp.zeros_like(l_ref)
    acc_ref[...] = jnp.zeros_like(acc_ref)

    last = n_chunks - 1

    def stage(c, dst_ref):
        bias = scr_ref[c]
        for h in range(N_HEADS):
            dst_ref[h] = bias + _dot(k_ref[0, h // Q_PER_KV, c],
                                     qT_ref[0, h * HEAD_DIM:(h + 1) * HEAD_DIM, :])

    def softmax_step(c, src_ref):
        for h in range(N_HEADS):
            g = h // Q_PER_KV
            m_old = m_ref[h]
            m_new = jnp.maximum(m_old, jnp.max(src_ref[h], axis=0, keepdims=True))
            m_safe = jnp.where(m_new == neg_inf, 0.0, m_new)
            corr = jnp.exp2(m_old - m_safe)
            p = jnp.exp2(src_ref[h] - m_safe)
            l_ref[h] = l_ref[h] * corr + jnp.sum(p, axis=0, keepdims=True)
            acc_ref[h] = acc_ref[h] * corr + _dot(vT_ref[0, g, c], p.astype(jnp.bfloat16))
            m_ref[h] = m_new

    stage(0, sa_ref)

    def attend_pair(i, carry):
        c_even = 2 * i
        stage(jnp.minimum(c_even + 1, last), sb_ref)
        softmax_step(c_even, sa_ref)

        @pl.when(c_even + 1 <= last)
        def _():
            stage(jnp.minimum(c_even + 2, last), sa_ref)
            softmax_step(c_even + 1, sb_ref)
        return carry

    lax.fori_loop(0, (n_chunks + 1) // 2, attend_pair, 0)
    for h in range(N_HEADS):
        acc_ref[h] = acc_ref[h] / l_ref[h]

    outT = acc_ref[...].reshape(N_HEADS * HEAD_DIM, TQ)
    out_ref[0] = outT.T.astype(jnp.bfloat16)


def _dsa_attention(qT, qiT, wiT, k, ki, vT):
    B, _, S = qT.shape
    k = k.reshape(B, N_KV_HEADS, S // TK, TK, HEAD_DIM)
    ki = ki.reshape(B, S // TK, TK, IDX_DIM)
    per_batch = lambda a: pl.BlockSpec((1,) + a.shape[1:], lambda b, i: (b,) + (0,) * (a.ndim - 1))
    return pl.pallas_call(
        _dsa_kernel,
        grid=(B, S // TQ),
        in_specs=[
            pl.BlockSpec((1, Q_COLS, TQ), lambda b, i: (b, 0, i)),
            pl.BlockSpec((1, QI_COLS, TQ), lambda b, i: (b, 0, i)),
            pl.BlockSpec((1, IDX_HEADS, TQ), lambda b, i: (b, 0, i)),
            per_batch(k), per_batch(ki), per_batch(vT),
        ],
        out_specs=pl.BlockSpec((1, TQ, Q_COLS), lambda b, i: (b, i, 0)),
        out_shape=jax.ShapeDtypeStruct((B, S, Q_COLS), jnp.bfloat16),
        scratch_shapes=[
            pltpu.VMEM((S // TK, TK, TQ), jnp.float32),
            pltpu.VMEM((N_HEADS, TK, TQ), jnp.float32),
            pltpu.VMEM((N_HEADS, TK, TQ), jnp.float32),
            pltpu.VMEM((N_HEADS, 1, TQ), jnp.float32),
            pltpu.VMEM((N_HEADS, 1, TQ), jnp.float32),
            pltpu.VMEM((N_HEADS, HEAD_DIM, TQ), jnp.float32),
        ],
        compiler_params=pltpu.CompilerParams(
            dimension_semantics=("arbitrary", "arbitrary"), vmem_limit_bytes=VMEM_LIMIT_BYTES),
        name="dsa_attention",
    )(qT, qiT, wiT, k, ki, vT)


def _layer_norm(z, g, b):
    mu = jnp.mean(z, axis=-1, keepdims=True)
    zc = z - mu
    var = jnp.mean(zc * zc, axis=-1, keepdims=True)
    return zc * lax.rsqrt(var + LN_EPS) * g + b


def _router_gates(x1, hi, wr_ref, br_ref):
    lo = (x1 - hi.astype(jnp.float32)).astype(jnp.bfloat16)

    r = _dot(hi, wr_ref[...])
    logits = r[:, :LANES] + r[:, LANES:] + _dot(lo, wr_ref[:, :LANES]) + br_ref[...]

    tm = logits.shape[0]
    lane = lax.broadcasted_iota(jnp.int32, (tm, LANES), 1)
    neg_inf = -jnp.inf
    is_grp = jnp.where(lane >= N_EXPERTS, jnp.where(lane < N_EXPERTS + N_EXPERT_GROUPS, 1.0, 0.0), 0.0)
    gl = jnp.where(is_grp > 0.0, logits, neg_inf)
    g_max = jnp.max(gl, axis=1, keepdims=True)
    g_sel = jnp.min(jnp.where(gl == g_max, lane, LANES), axis=1, keepdims=True) - N_EXPERTS
    g_weight = 1.0 / jnp.sum(jnp.exp(gl - g_max), axis=1, keepdims=True)

    in_grp = jnp.right_shift(lane, EXPERTS_PER_GROUP.bit_length() - 1) == g_sel
    el = jnp.where(in_grp, logits, neg_inf)
    v1 = jnp.max(el, axis=1, keepdims=True)
    i1 = jnp.min(jnp.where(el == v1, lane, LANES), axis=1, keepdims=True)
    el2 = jnp.where(lane == i1, neg_inf, el)
    v2 = jnp.max(el2, axis=1, keepdims=True)
    i2 = jnp.min(jnp.where(el2 == v2, lane, LANES), axis=1, keepdims=True)
    e2 = jnp.exp(v2 - v1)
    w1 = g_weight / (1.0 + e2)
    w2 = g_weight * e2 / (1.0 + e2)
    return jnp.where(lane == i1, w1, jnp.where(lane == i2, w2, 0.0))


def _tail_kernel(attn_ref, pool_ref, x_ref, woa_ref, wop_ref, g1_ref, b1_ref, wr_ref, br_ref,
                 wg_ref, wu_ref, wd_ref, g2_ref, b2_ref, out_ref, hid_ref):
    mix = _dot(attn_ref[...], woa_ref[...]) + _dot(pool_ref[...], wop_ref[...])
    x1 = _layer_norm(ALPHA * x_ref[...] + mix, g1_ref[...], b1_ref[...])
    xb = x1.astype(jnp.bfloat16)
    gate = _router_gates(x1, xb, wr_ref, br_ref)

    lane = lax.broadcasted_iota(jnp.int32, gate.shape, 1)
    for e in range(N_EXPERTS):
        a = _dot(xb, wg_ref[e])
        b = _dot(xb, wu_ref[e])
        gcol = jnp.sum(jnp.where(lane == e, gate, 0.0), axis=1, keepdims=True)
        hid = (a * jax.nn.sigmoid(a)) * b * gcol
        hid_ref[:, e * D_EXPERT:(e + 1) * D_EXPERT] = hid.astype(jnp.bfloat16)
    y = _dot(hid_ref[...], wd_ref[...])
    out_ref[...] = _layer_norm(ALPHA * x1 + y, g2_ref[...], b2_ref[...])


def _output_and_experts(attn, pool, x2d, w_out, ln1_g, ln1_b, w_gr, b_gr, w_er, b_er,
                        w_gate, w_up, w_down, ln2_g, ln2_b):
    T, D = x2d.shape
    tm = TM_TAIL
    bf = jnp.bfloat16
    woa = w_out[:Q_COLS].astype(bf)
    wop = w_out[Q_COLS:].astype(bf)
    w_r = jnp.pad(jnp.concatenate([w_er, w_gr], axis=1), ((0, 0), (0, LANES - N_EXPERTS - N_EXPERT_GROUPS)))
    w_r_hi = w_r.astype(bf)
    w_r_lo = (w_r - w_r_hi.astype(jnp.float32)).astype(bf)
    wr = jnp.concatenate([w_r_hi, w_r_lo], axis=1)
    br = jnp.pad(jnp.concatenate([b_er, b_gr]), (0, LANES - N_EXPERTS - N_EXPERT_GROUPS)).reshape(1, LANES)
    wg = w_gate.reshape(N_EXPERTS, D, D_EXPERT).astype(bf)
    wu = w_up.reshape(N_EXPERTS, D, D_EXPERT).astype(bf)
    wd = w_down.reshape(N_EXPERTS * D_EXPERT, D).astype(bf)
    resident = lambda a: pl.BlockSpec(a.shape, lambda i: (0,) * a.ndim, pipeline_mode=pl.Buffered(1))
    consts = (woa, wop, ln1_g.reshape(1, D), ln1_b.reshape(1, D), wr, br,
              wg, wu, wd, ln2_g.reshape(1, D), ln2_b.reshape(1, D))
    return pl.pallas_call(
        _tail_kernel,
        grid=(T // tm,),
        in_specs=[pl.BlockSpec((tm, Q_COLS), lambda i: (i, 0)),
                  pl.BlockSpec((tm, D_POOL), lambda i: (i, 0)),
                  pl.BlockSpec((tm, D), lambda i: (i, 0))] + [resident(c) for c in consts],
        out_specs=pl.BlockSpec((tm, D), lambda i: (i, 0)),
        out_shape=jax.ShapeDtypeStruct((T, D), jnp.float32),
        scratch_shapes=[pltpu.VMEM((tm, N_EXPERTS * D_EXPERT), bf)],
        compiler_params=pltpu.CompilerParams(
            dimension_semantics=("arbitrary",), vmem_limit_bytes=VMEM_LIMIT_BYTES),
        name="dsa_output_and_experts",
    )(attn, pool, x2d, *consts)


def kernel(x, w_in, w_pool, pool_scale, w_out, ln1_g, ln1_b, w_group_router, b_group_router,
           w_expert_router, b_expert_router, w_gate, w_up, w_down, ln2_g, ln2_b):
    B, S, D = x.shape
    assert D == D_MODEL and S % TM_PROJ == 0 and S % TQ == 0 and TQ == TK
    assert w_in.shape[0] == DEPTH == 1
    l = 0
    qT, vT, qiT, wiT, k, ki, pool = _input_projection(x, w_in[l], w_pool[l], pool_scale[l])
    attn = _dsa_attention(qT, qiT, wiT, k, ki, vT)
    x2d = x.reshape(B * S, D)
    out = _output_and_experts(
        attn.reshape(B * S, Q_COLS), pool.reshape(B * S, D_POOL), x2d, w_out[l], ln1_g[l], ln1_b[l],
        w_group_router[l], b_group_router[l], w_expert_router[l], b_expert_router[l],
        w_gate[l], w_up[l], w_down[l], ln2_g[l], ln2_b[l])
    return out.reshape(B, S, D)
```

```python
import functools
import math

import jax
import jax.numpy as jnp
from jax import lax
from jax.experimental import pallas as pl
from jax.experimental.pallas import tpu as pltpu

D_MODEL = 1024
N_HEADS = 8
HEAD_DIM = 64
N_KV_HEADS = 2
Q_PER_KV = N_HEADS // N_KV_HEADS
IDX_HEADS = 8
IDX_DIM = 64
TOPK_MAX = 256
POOL_WINDOWS = (2, 4, 8, 16)
N_POOL_GROUPS = len(POOL_WINDOWS)
POOL_GROUP_DIM = 128
D_POOL = N_POOL_GROUPS * POOL_GROUP_DIM
MAX_WIN = max(POOL_WINDOWS)
Q_COLS = N_HEADS * HEAD_DIM
KV_COLS = N_KV_HEADS * HEAD_DIM
QI_COLS = IDX_HEADS * IDX_DIM
N_EXPERT_GROUPS = 4
EXPERTS_PER_GROUP = 4
N_EXPERTS = N_EXPERT_GROUPS * EXPERTS_PER_GROUP
D_EXPERT = 256
DEPTH = 1
ALPHA = float((2 * DEPTH) ** 0.25)
LN_EPS = 1e-5

LANES = 128
SUBLANES = 8
MXU_DIM = 256
VMEM_LIMIT_BYTES = 56 * 1024 * 1024

TM_PROJ = 512
TQ = MXU_DIM
TK = MXU_DIM
TK_PIECE = LANES
TM_TAIL = 512

F32_MAX = float(jnp.finfo(jnp.float32).max)
KEEP_ALL = 1.0e9
PACKED_SUBLANES = 2 * SUBLANES
BF16_ULP_BOUND = 2.0 ** -7
F32_TINY = 2.0 ** -120
COARSE_ITERS = 10
FINE_ITERS_UNCHECKED = 6
BISECT_MAX_ITERS = 40

_NT_DIMS = (((1,), (1,)), ((), ()))


def _dot(a, b):
    return jnp.dot(a, b, preferred_element_type=jnp.float32)


def _dot_nt(a, b):
    return lax.dot_general(a, b, _NT_DIMS, preferred_element_type=jnp.float32)


def _proj_kernel(x_ref, wqT_ref, wvT_ref, wqiT_ref, wwiT_ref, wk0_ref, wk1_ref, wki_ref, wpin_ref,
                 wpool_ref, pscale_ref,
                 qT_ref, vT_ref, qiT_ref, wiT_ref, k_ref, ki_ref, pool_ref,
                 hist_ref, ext_ref, *, idx_scale):
    s_blk = pl.program_id(1)
    tm = x_ref.shape[1]
    xb = x_ref[0].astype(jnp.bfloat16)

    qT_ref[0] = _dot_nt(wqT_ref[...], xb).astype(jnp.bfloat16)
    qiT_ref[0] = _dot_nt(wqiT_ref[...], xb).astype(jnp.bfloat16)
    wiT_ref[0] = _dot_nt(wwiT_ref[...], xb)[:IDX_HEADS] * idx_scale
    vT = _dot_nt(wvT_ref[...], xb).astype(jnp.bfloat16)
    for g in range(N_KV_HEADS):
        for j in range(tm // TK):
            vT_ref[0, g, j] = vT[g * HEAD_DIM:(g + 1) * HEAD_DIM, j * TK:(j + 1) * TK]

    k_ref[0, 0] = _dot(xb, wk0_ref[...]).astype(jnp.bfloat16)
    k_ref[0, 1] = _dot(xb, wk1_ref[...]).astype(jnp.bfloat16)
    ki_ref[0] = _dot(xb, wki_ref[...]).astype(jnp.bfloat16)

    u = _dot(xb, wpin_ref[...])

    @pl.when(s_blk == 0)
    def _():
        hist_ref[...] = jnp.zeros_like(hist_ref)

    ext_ref[0:MAX_WIN, :] = hist_ref[...]
    ext_ref[MAX_WIN:MAX_WIN + tm, :] = u
    hist_ref[...] = u[tm - MAX_WIN:tm, :]

    pos = s_blk * tm + lax.broadcasted_iota(jnp.int32, (tm, POOL_GROUP_DIM), 0)
    for g, win in enumerate(POOL_WINDOWS):
        c0 = g * POOL_GROUP_DIM
        acc = ext_ref[MAX_WIN:MAX_WIN + tm, c0:c0 + POOL_GROUP_DIM]
        for j in range(1, win):
            acc = acc + ext_ref[MAX_WIN - j:MAX_WIN - j + tm, c0:c0 + POOL_GROUP_DIM]
        count = jnp.minimum(pos + 1, win).astype(jnp.float32)
        d = acc / count - ext_ref[MAX_WIN:MAX_WIN + tm, c0:c0 + POOL_GROUP_DIM]
        y = _dot(d.astype(jnp.bfloat16), wpool_ref[g]) * pscale_ref[:, c0:c0 + POOL_GROUP_DIM]
        pool_ref[0, :, c0:c0 + POOL_GROUP_DIM] = y.astype(jnp.bfloat16)


def _input_projection(x, w_in, w_pool, pool_scale):
    B, S, D = x.shape
    tm = TM_PROJ
    bf = jnp.bfloat16
    idx_scale = (IDX_DIM ** -0.5) * (IDX_HEADS ** -0.5)
    q_scale = (HEAD_DIM ** -0.5) * math.log2(math.e)

    o = 0
    w_q = w_in[:, o:o + Q_COLS]; o += Q_COLS
    w_k = w_in[:, o:o + KV_COLS]; o += KV_COLS
    w_v = w_in[:, o:o + KV_COLS]; o += KV_COLS
    w_pin = w_in[:, o:o + D_POOL]; o += D_POOL
    w_qi = w_in[:, o:o + QI_COLS]; o += QI_COLS
    w_ki = w_in[:, o:o + IDX_DIM]; o += IDX_DIM
    w_wi = w_in[:, o:o + IDX_HEADS]

    wqT = (w_q * q_scale).T.astype(bf)
    wvT = w_v.T.astype(bf)
    wqiT = w_qi.T.astype(bf)
    wwiT = jnp.pad(w_wi.T, ((0, 2 * SUBLANES - IDX_HEADS), (0, 0))).astype(bf)
    wk0 = w_k[:, :HEAD_DIM].astype(bf)
    wk1 = w_k[:, HEAD_DIM:].astype(bf)

    full = lambda a: pl.BlockSpec(a.shape, lambda b, s: (0,) * a.ndim)
    weights = (wqT, wvT, wqiT, wwiT, wk0, wk1, w_ki.astype(bf), w_pin.astype(bf),
               w_pool.astype(bf), pool_scale.reshape(1, D_POOL))
    out_shape = (
        jax.ShapeDtypeStruct((B, Q_COLS, S), bf),
        jax.ShapeDtypeStruct((B, N_KV_HEADS, S // TK, HEAD_DIM, TK), bf),
        jax.ShapeDtypeStruct((B, QI_COLS, S), bf),
        jax.ShapeDtypeStruct((B, IDX_HEADS, S), jnp.float32),
        jax.ShapeDtypeStruct((B, N_KV_HEADS, S, HEAD_DIM), bf),
        jax.ShapeDtypeStruct((B, S, IDX_DIM), bf),
        jax.ShapeDtypeStruct((B, S, D_POOL), bf),
    )
    out_specs = (
        pl.BlockSpec((1, Q_COLS, tm), lambda b, s: (b, 0, s)),
        pl.BlockSpec((1, N_KV_HEADS, tm // TK, HEAD_DIM, TK), lambda b, s: (b, 0, s, 0, 0)),
        pl.BlockSpec((1, QI_COLS, tm), lambda b, s: (b, 0, s)),
        pl.BlockSpec((1, IDX_HEADS, tm), lambda b, s: (b, 0, s)),
        pl.BlockSpec((1, N_KV_HEADS, tm, HEAD_DIM), lambda b, s: (b, 0, s, 0)),
        pl.BlockSpec((1, tm, IDX_DIM), lambda b, s: (b, s, 0)),
        pl.BlockSpec((1, tm, D_POOL), lambda b, s: (b, s, 0)),
    )
    return pl.pallas_call(
        functools.partial(_proj_kernel, idx_scale=idx_scale),
        grid=(B, S // tm),
        in_specs=[pl.BlockSpec((1, tm, D), lambda b, s: (b, s, 0))] + [full(w) for w in weights],
        out_specs=out_specs,
        out_shape=out_shape,
        scratch_shapes=[pltpu.VMEM((MAX_WIN, D_POOL), jnp.float32),
                        pltpu.VMEM((MAX_WIN + tm, D_POOL), jnp.float32)],
        compiler_params=pltpu.CompilerParams(
            dimension_semantics=("arbitrary", "arbitrary"), vmem_limit_bytes=VMEM_LIMIT_BYTES),
        name="dsa_input_projection",
    )(x, *weights)


def _dsa_kernel(qT_ref, qiT_ref, wiT_ref, k_ref, ki_ref, vT_ref, out_ref,
                scr_ref, s16_ref, sa_ref, sb_ref, m_ref, l_ref, acc_ref):
    q_blk = pl.program_id(1)
    n_chunks = q_blk + 1
    q0 = q_blk * TQ
    topk = float(TOPK_MAX)
    neg_inf = -jnp.inf

    row = lambda v: jnp.full((1, TQ), v, jnp.float32)
    flag = lambda cond: jnp.where(cond, 1.0, 0.0)
    qpos = q0 + lax.broadcasted_iota(jnp.int32, (1, TQ), 1)
    kiota = lax.broadcasted_iota(jnp.int32, (TK, TQ), 0)

    def load_chunk(c):
        return scr_ref[c]

    def store_chunk(c, val):
        scr_ref[c] = val

    N_ACC = 4

    def fold(x):
        parts = x.reshape(N_ACC, x.shape[0] // (N_ACC * SUBLANES), SUBLANES, TQ)
        return tuple(jnp.sum(parts[i], axis=0) for i in range(N_ACC))

    def add_folded(accs, x):
        return tuple(a + f for a, f in zip(accs, fold(x)))

    def finish(accs):
        return jnp.sum(functools.reduce(lambda a, b: a + b, accs), axis=0, keepdims=True)

    zero_accs = tuple(jnp.zeros((SUBLANES, TQ), jnp.float32) for _ in range(N_ACC))

    def score_piece(c, r0, carry, diagonal):
        mn, mx, a_ge, a_gt = carry
        rows = slice(r0, r0 + TK_PIECE)
        kic = ki_ref[0, c, rows, :]
        sc = jnp.zeros((TK_PIECE, TQ), jnp.float32)
        for h in range(IDX_HEADS):
            z = _dot(kic, qiT_ref[0, h * IDX_DIM:(h + 1) * IDX_DIM, :])
            sc = sc + jnp.maximum(z, 0.0) * wiT_ref[0, h:h + 1, :]
        sc_hi = sc
        if diagonal:
            valid = (q0 + r0 + lax.broadcasted_iota(jnp.int32, (TK_PIECE, TQ), 0)) <= qpos
            sc_hi = jnp.where(valid, sc, jnp.inf)
            sc = jnp.where(valid, sc, neg_inf)
        scr_ref[c, rows, :] = sc
        s16_ref[c, rows, :] = sc.astype(jnp.bfloat16)
        mx = jnp.maximum(mx, jnp.max(sc, axis=0, keepdims=True))
        mn = jnp.minimum(mn, jnp.min(sc_hi, axis=0, keepdims=True))
        return mn, mx, add_folded(a_ge, flag(sc >= 0.0)), add_folded(a_gt, flag(sc > 0.0))

    def score_chunk(c, carry, diagonal=False):
        for r0 in range(0, TK, TK_PIECE):
            carry = score_piece(c, r0, carry, diagonal)
        return carry

    carry = lax.fori_loop(0, q_blk, score_chunk, (row(jnp.inf), row(neg_inf), zero_accs, zero_accs))
    smin, smax, a_ge, a_gt = score_chunk(q_blk, carry, diagonal=True)
    c_ge0 = finish(a_ge)
    c_gt0 = finish(a_gt)

    def count_ge(p):
        def body(c, accs):
            return add_folded(accs, flag(load_chunk(c) >= p))
        return finish(lax.fori_loop(0, n_chunks, body, zero_accs))

    n_valid = (qpos + 1).astype(jnp.float32)
    short = flag(n_valid <= topk)
    at_zero = (1.0 - short) * flag(c_ge0 >= topk) * flag(c_gt0 <= topk)
    above = (1.0 - short) * flag(c_gt0 > topk)
    below = (1.0 - short) * flag(c_ge0 < topk)

    thr = jnp.where(short > 0.0, -F32_MAX, 0.0)
    keep = jnp.where(at_zero * flag(c_ge0 > topk) > 0.0, topk - c_gt0, KEEP_ALL)
    done = jnp.maximum(short, at_zero)
    lo = jnp.where(above > 0.0, 0.0, smin)
    hi = jnp.where(below > 0.0, 0.0, jnp.inf)
    c_hi = jnp.where(below > 0.0, c_ge0, 0.0)

    def coarse_step(_, st):
        lo, hi = st
        p16 = (0.5 * lo + 0.5 * jnp.minimum(hi, smax)).astype(jnp.bfloat16)
        p = p16.astype(jnp.float32)
        inside = flag(p > lo) * flag(p < hi)
        def body(c, accs):
            hit = jnp.where(s16_ref[c] >= p16, one16, zero16)
            parts = hit.reshape(N_ACC, TK // (N_ACC * PACKED_SUBLANES), PACKED_SUBLANES, TQ)
            return tuple(functools.reduce(lambda x, y: x + y, [a] + [parts[i, r] for r in range(parts.shape[1])])
                         for i, a in enumerate(accs))
        accs = lax.fori_loop(0, n_chunks, body, zero_accs16)
        c = finish(tuple(a.astype(jnp.float32) for a in accs))
        lo = jnp.where(inside * flag(c >= topk) > 0.0, p, lo)
        hi = jnp.where(inside * flag(c < topk) > 0.0, p, hi)
        return lo, hi

    one16 = jnp.ones((), jnp.bfloat16)
    zero16 = jnp.zeros((), jnp.bfloat16)
    zero_accs16 = tuple(jnp.zeros((PACKED_SUBLANES, TQ), jnp.bfloat16) for _ in range(N_ACC))
    lo, hi_coarse = lax.fori_loop(0, COARSE_ITERS, coarse_step, (lo, hi))
    lo = lo - (jnp.abs(lo) * BF16_ULP_BOUND + F32_TINY)

    def n_open(done, parked):
        return jnp.sum(flag(done + parked <= 0.0))

    def probe(p, active, lo, hi, c_hi, thr, done):
        c = count_ge(p)
        is_eq = active * flag(c == topk)
        is_gt = active * flag(c > topk)
        is_lt = active * flag(c < topk)
        thr = jnp.where(is_eq > 0.0, p, thr)
        done = jnp.maximum(done, is_eq)
        lo = jnp.where(is_gt > 0.0, p, lo)
        hi = jnp.where(is_lt > 0.0, p, hi)
        c_hi = jnp.where(is_lt > 0.0, c, c_hi)
        return lo, hi, c_hi, thr, done

    def bisect_step(lo, hi, c_hi, thr, done, parked):
        p = 0.5 * lo + 0.5 * jnp.minimum(hi, smax)
        parked = jnp.maximum(parked, 1.0 - flag(p > lo) * flag(p < hi))
        parked = jnp.maximum(parked, flag(c_hi == topk - 1.0))
        active = flag(done + parked <= 0.0)
        return probe(p, active, lo, hi, c_hi, thr, done) + (parked,)

    lo, hi, c_hi, thr, done = probe(jnp.minimum(hi_coarse, smax), 1.0 - done, lo, hi, c_hi, thr, done)
    vec = lax.fori_loop(0, FINE_ITERS_UNCHECKED, lambda _, v: bisect_step(*v),
                        (lo, hi, c_hi, thr, done, row(0.0)))

    def bisect_cond(st):
        return jnp.logical_and(st[0] < BISECT_MAX_ITERS, st[1] > 0.0)

    def bisect_body(st):
        vec = bisect_step(*bisect_step(*st[2:]))
        return (st[0] + 2, n_open(vec[4], vec[5])) + vec

    st = (jnp.int32(0), n_open(vec[4], vec[5])) + vec
    _, _, _, hi, c_hi, thr, done, _ = lax.while_loop(bisect_cond, bisect_body, st)

    def walk_cond(st):
        return st[0] > 0.0

    def walk_body(st):
        _, hi, c_hi, thr, keep, done = st
        def below_max(c, acc):
            blk = load_chunk(c)
            part = jnp.where(blk < hi, blk, neg_inf)
            return jnp.maximum(acc, jnp.max(part.reshape(TK // SUBLANES, SUBLANES, TQ), axis=0))
        m8 = lax.fori_loop(0, n_chunks, below_max, jnp.full((SUBLANES, TQ), neg_inf, jnp.float32))
        m = jnp.max(m8, axis=0, keepdims=True)
        c = count_ge(m)
        active = 1.0 - done
        reach = active * flag(c >= topk)
        thr = jnp.where(reach > 0.0, m, thr)
        keep = jnp.where(reach * flag(c > topk) > 0.0, topk - c_hi, keep)
        done = jnp.maximum(done, reach)
        step = active * (1.0 - reach)
        hi = jnp.where(step > 0.0, m, hi)
        c_hi = jnp.where(step > 0.0, c, c_hi)
        return (jnp.sum(1.0 - done), hi, c_hi, thr, keep, done)

    st = (jnp.sum(1.0 - done), hi, c_hi, thr, keep, done)
    _, _, _, thr, keep, _ = lax.while_loop(walk_cond, walk_body, st)

    has_ties = jnp.sum(flag(keep < KEEP_ALL)) > 0.0

    @pl.when(jnp.logical_not(has_ties))
    def _():
        def body(c, carry):
            store_chunk(c, jnp.where(load_chunk(c) >= thr, 0.0, neg_inf))
            return carry
        lax.fori_loop(0, n_chunks, body, 0)

    @pl.when(has_ties)
    def _():
        tri = flag(lax.broadcasted_iota(jnp.int32, (TK, TK), 1)
                   <= lax.broadcasted_iota(jnp.int32, (TK, TK), 0)).astype(jnp.bfloat16)
        def body(c, seen):
            blk = load_chunk(c)
            eq = flag(blk == thr)
            rank = seen + _dot(tri, eq.astype(jnp.bfloat16))
            sel = jnp.maximum(flag(blk > thr), eq * flag(rank <= keep))
            store_chunk(c, jnp.where(sel > 0.0, 0.0, neg_inf))
            return seen + finish(fold(eq))
        lax.fori_loop(0, n_chunks, body, row(0.0))

    m_ref[...] = jnp.full_like(m_ref, neg_inf)
    l_ref[...] = jnp.zeros_like(l_ref)
    acc_ref[...] = jnp.zeros_like(acc_ref)

    last = n_chunks - 1

    def stage(c, dst_ref):
        bias = scr_ref[c]
        for h in range(N_HEADS):
            dst_ref[h] = bias + _dot(k_ref[0, h // Q_PER_KV, c],
                                     qT_ref[0, h * HEAD_DIM:(h + 1) * HEAD_DIM, :])

    def softmax_step(c, src_ref):
        for h in range(N_HEADS):
            g = h // Q_PER_KV
            m_old = m_ref[h]
            m_new = jnp.maximum(m_old, jnp.max(src_ref[h], axis=0, keepdims=True))
            m_safe = jnp.where(m_new == neg_inf, 0.0, m_new)
            corr = jnp.exp2(m_old - m_safe)
            p = jnp.exp2(src_ref[h] - m_safe)
            l_ref[h] = l_ref[h] * corr + jnp.sum(p, axis=0, keepdims=True)
            acc_ref[h] = acc_ref[h] * corr + _dot(vT_ref[0, g, c], p.astype(jnp.bfloat16))
            m_ref[h] = m_new

    stage(0, sa_ref)

    def attend_pair(i, carry):
        c_even = 2 * i
        stage(jnp.minimum(c_even + 1, last), sb_ref)
        softmax_step(c_even, sa_ref)

        @pl.when(c_even + 1 <= last)
        def _():
            stage(jnp.minimum(c_even + 2, last), sa_ref)
            softmax_step(c_even + 1, sb_ref)
        return carry

    lax.fori_loop(0, (n_chunks + 1) // 2, attend_pair, 0)
    for h in range(N_HEADS):
        acc_ref[h] = acc_ref[h] / l_ref[h]

    outT = acc_ref[...].reshape(N_HEADS * HEAD_DIM, TQ)
    out_ref[0] = outT.T.astype(jnp.bfloat16)


def _dsa_attention(qT, qiT, wiT, k, ki, vT):
    B, _, S = qT.shape
    k = k.reshape(B, N_KV_HEADS, S // TK, TK, HEAD_DIM)
    ki = ki.reshape(B, S // TK, TK, IDX_DIM)
    per_batch = lambda a: pl.BlockSpec((1,) + a.shape[1:], lambda b, i: (b,) + (0,) * (a.ndim - 1))
    return pl.pallas_call(
        _dsa_kernel,
        grid=(B, S // TQ),
        in_specs=[
            pl.BlockSpec((1, Q_COLS, TQ), lambda b, i: (b, 0, i)),
            pl.BlockSpec((1, QI_COLS, TQ), lambda b, i: (b, 0, i)),
            pl.BlockSpec((1, IDX_HEADS, TQ), lambda b, i: (b, 0, i)),
            per_batch(k), per_batch(ki), per_batch(vT),
        ],
        out_specs=pl.BlockSpec((1, TQ, Q_COLS), lambda b, i: (b, i, 0)),
        out_shape=jax.ShapeDtypeStruct((B, S, Q_COLS), jnp.bfloat16),
        scratch_shapes=[
            pltpu.VMEM((S // TK, TK, TQ), jnp.float32),
            pltpu.VMEM((S // TK, TK, TQ), jnp.bfloat16),
            pltpu.VMEM((N_HEADS, TK, TQ), jnp.float32),
            pltpu.VMEM((N_HEADS, TK, TQ), jnp.float32),
            pltpu.VMEM((N_HEADS, 1, TQ), jnp.float32),
            pltpu.VMEM((N_HEADS, 1, TQ), jnp.float32),
            pltpu.VMEM((N_HEADS, HEAD_DIM, TQ), jnp.float32),
        ],
        compiler_params=pltpu.CompilerParams(
            dimension_semantics=("arbitrary", "arbitrary"), vmem_limit_bytes=VMEM_LIMIT_BYTES),
        name="dsa_attention",
    )(qT, qiT, wiT, k, ki, vT)


def _layer_norm(z, g, b):
    mu = jnp.mean(z, axis=-1, keepdims=True)
    zc = z - mu
    var = jnp.mean(zc * zc, axis=-1, keepdims=True)
    return zc * lax.rsqrt(var + LN_EPS) * g + b


def _router_gates(x1, hi, wr_ref, br_ref):
    lo = (x1 - hi.astype(jnp.float32)).astype(jnp.bfloat16)

    r = _dot(hi, wr_ref[...])
    logits = r[:, :LANES] + r[:, LANES:] + _dot(lo, wr_ref[:, :LANES]) + br_ref[...]

    tm = logits.shape[0]
    lane = lax.broadcasted_iota(jnp.int32, (tm, LANES), 1)
    neg_inf = -jnp.inf
    is_grp = jnp.where(lane >= N_EXPERTS, jnp.where(lane < N_EXPERTS + N_EXPERT_GROUPS, 1.0, 0.0), 0.0)
    gl = jnp.where(is_grp > 0.0, logits, neg_inf)
    g_max = jnp.max(gl, axis=1, keepdims=True)
    g_sel = jnp.min(jnp.where(gl == g_max, lane, LANES), axis=1, keepdims=True) - N_EXPERTS
    g_weight = 1.0 / jnp.sum(jnp.exp(gl - g_max), axis=1, keepdims=True)

    in_grp = jnp.right_shift(lane, EXPERTS_PER_GROUP.bit_length() - 1) == g_sel
    el = jnp.where(in_grp, logits, neg_inf)
    v1 = jnp.max(el, axis=1, keepdims=True)
    i1 = jnp.min(jnp.where(el == v1, lane, LANES), axis=1, keepdims=True)
    el2 = jnp.where(lane == i1, neg_inf, el)
    v2 = jnp.max(el2, axis=1, keepdims=True)
    i2 = jnp.min(jnp.where(el2 == v2, lane, LANES), axis=1, keepdims=True)
    e2 = jnp.exp(v2 - v1)
    w1 = g_weight / (1.0 + e2)
    w2 = g_weight * e2 / (1.0 + e2)
    return jnp.where(lane == i1, w1, jnp.where(lane == i2, w2, 0.0))


def _tail_kernel(attn_ref, pool_ref, x_ref, woa_ref, wop_ref, g1_ref, b1_ref, wr_ref, br_ref,
                 wg_ref, wu_ref, wd_ref, g2_ref, b2_ref, out_ref, hid_ref):
    mix = _dot(attn_ref[...], woa_ref[...]) + _dot(pool_ref[...], wop_ref[...])
    x1 = _layer_norm(ALPHA * x_ref[...] + mix, g1_ref[...], b1_ref[...])
    xb = x1.astype(jnp.bfloat16)
    gate = _router_gates(x1, xb, wr_ref, br_ref)

    lane = lax.broadcasted_iota(jnp.int32, gate.shape, 1)
    for e in range(N_EXPERTS):
        a = _dot(xb, wg_ref[e])
        b = _dot(xb, wu_ref[e])
        gcol = jnp.sum(jnp.where(lane == e, gate, 0.0), axis=1, keepdims=True)
        hid = (a * jax.nn.sigmoid(a)) * b * gcol
        hid_ref[:, e * D_EXPERT:(e + 1) * D_EXPERT] = hid.astype(jnp.bfloat16)
    y = _dot(hid_ref[...], wd_ref[...])
    out_ref[...] = _layer_norm(ALPHA * x1 + y, g2_ref[...], b2_ref[...])


def _output_and_experts(attn, pool, x2d, w_out, ln1_g, ln1_b, w_gr, b_gr, w_er, b_er,
                        w_gate, w_up, w_down, ln2_g, ln2_b):
    T, D = x2d.shape
    tm = TM_TAIL
    bf = jnp.bfloat16
    woa = w_out[:Q_COLS].astype(bf)
    wop = w_out[Q_COLS:].astype(bf)
    w_r = jnp.pad(jnp.concatenate([w_er, w_gr], axis=1), ((0, 0), (0, LANES - N_EXPERTS - N_EXPERT_GROUPS)))
    w_r_hi = w_r.astype(bf)
    w_r_lo = (w_r - w_r_hi.astype(jnp.float32)).astype(bf)
    wr = jnp.concatenate([w_r_hi, w_r_lo], axis=1)
    br = jnp.pad(jnp.concatenate([b_er, b_gr]), (0, LANES - N_EXPERTS - N_EXPERT_GROUPS)).reshape(1, LANES)
    wg = w_gate.reshape(N_EXPERTS, D, D_EXPERT).astype(bf)
    wu = w_up.reshape(N_EXPERTS, D, D_EXPERT).astype(bf)
    wd = w_down.reshape(N_EXPERTS * D_EXPERT, D).astype(bf)
    resident = lambda a: pl.BlockSpec(a.shape, lambda i: (0,) * a.ndim, pipeline_mode=pl.Buffered(1))
    consts = (woa, wop, ln1_g.reshape(1, D), ln1_b.reshape(1, D), wr, br,
              wg, wu, wd, ln2_g.reshape(1, D), ln2_b.reshape(1, D))
    return pl.pallas_call(
        _tail_kernel,
        grid=(T // tm,),
        in_specs=[pl.BlockSpec((tm, Q_COLS), lambda i: (i, 0)),
                  pl.BlockSpec((tm, D_POOL), lambda i: (i, 0)),
                  pl.BlockSpec((tm, D), lambda i: (i, 0))] + [resident(c) for c in consts],
        out_specs=pl.BlockSpec((tm, D), lambda i: (i, 0)),
        out_shape=jax.ShapeDtypeStruct((T, D), jnp.float32),
        scratch_shapes=[pltpu.VMEM((tm, N_EXPERTS * D_EXPERT), bf)],
        compiler_params=pltpu.CompilerParams(
            dimension_semantics=("arbitrary",), vmem_limit_bytes=VMEM_LIMIT_BYTES),
        name="dsa_output_and_experts",
    )(attn, pool, x2d, *consts)


def kernel(x, w_in, w_pool, pool_scale, w_out, ln1_g, ln1_b, w_group_router, b_group_router,
           w_expert_router, b_expert_router, w_gate, w_up, w_down, ln2_g, ln2_b):
    B, S, D = x.shape
    assert D == D_MODEL and S % TM_PROJ == 0 and S % TQ == 0 and TQ == TK
    assert w_in.shape[0] == DEPTH == 1
    l = 0
    qT, vT, qiT, wiT, k, ki, pool = _input_projection(x, w_in[l], w_pool[l], pool_scale[l])
    attn = _dsa_attention(qT, qiT, wiT, k, ki, vT)
    x2d = x.reshape(B * S, D)
    out = _output_and_experts(
        attn.reshape(B * S, Q_COLS), pool.reshape(B * S, D_POOL), x2d, w_out[l], ln1_g[l], ln1_b[l],
        w_group_router[l], b_group_router[l], w_expert_router[l], b_expert_router[l],
        w_gate[l], w_up[l], w_down[l], ln2_g[l], ln2_b[l])
    return out.reshape(B, S, D)
```

```python
import functools
import math

import jax
import jax.numpy as jnp
from jax import lax
from jax.experimental import pallas as pl
from jax.experimental.pallas import tpu as pltpu

D_MODEL = 1024
N_HEADS = 8
HEAD_DIM = 64
N_KV_HEADS = 2
Q_PER_KV = N_HEADS // N_KV_HEADS
IDX_HEADS = 8
IDX_DIM = 64
TOPK_MAX = 256
POOL_WINDOWS = (2, 4, 8, 16)
N_POOL_GROUPS = len(POOL_WINDOWS)
POOL_GROUP_DIM = 128
D_POOL = N_POOL_GROUPS * POOL_GROUP_DIM
MAX_WIN = max(POOL_WINDOWS)
Q_COLS = N_HEADS * HEAD_DIM
KV_COLS = N_KV_HEADS * HEAD_DIM
QI_COLS = IDX_HEADS * IDX_DIM
N_EXPERT_GROUPS = 4
EXPERTS_PER_GROUP = 4
N_EXPERTS = N_EXPERT_GROUPS * EXPERTS_PER_GROUP
D_EXPERT = 256
DEPTH = 1
ALPHA = float((2 * DEPTH) ** 0.25)
LN_EPS = 1e-5

LANES = 128
SUBLANES = 8
MXU_DIM = 256
VMEM_LIMIT_BYTES = 56 * 1024 * 1024

TM_PROJ = 512
TQ = MXU_DIM
TK = MXU_DIM
TK_PIECE = LANES
STAGE_LEAD = 2
TM_TAIL = 512

F32_MAX = float(jnp.finfo(jnp.float32).max)
KEEP_ALL = 1.0e9
PACKED_SUBLANES = 2 * SUBLANES
BF16_ULP_BOUND = 2.0 ** -7
F32_TINY = 2.0 ** -120
COARSE_ITERS = 10
FINE_ITERS_UNCHECKED = 6
BISECT_MAX_ITERS = 40

_NT_DIMS = (((1,), (1,)), ((), ()))


def _dot(a, b):
    return jnp.dot(a, b, preferred_element_type=jnp.float32)


def _dot_nt(a, b):
    return lax.dot_general(a, b, _NT_DIMS, preferred_element_type=jnp.float32)


def _proj_kernel(x_ref, wqT_ref, wvT_ref, wqiT_ref, wwiT_ref, wnat_ref, wpool_ref, pscale_ref,
                 qT_ref, vT_ref, qiT_ref, wiT_ref, k_ref, ki_ref, pool_ref,
                 hist_ref, ext_ref, *, idx_scale):
    s_blk = pl.program_id(1)
    tm = x_ref.shape[1]
    xb = x_ref[0].astype(jnp.bfloat16)

    qT_ref[0] = _dot_nt(wqT_ref[...], xb).astype(jnp.bfloat16)
    qiT_ref[0] = _dot_nt(wqiT_ref[...], xb).astype(jnp.bfloat16)
    wiT_ref[0] = _dot_nt(wwiT_ref[...], xb)[:IDX_HEADS] * idx_scale
    vT = _dot_nt(wvT_ref[...], xb).astype(jnp.bfloat16)
    for g in range(N_KV_HEADS):
        for j in range(tm // TK):
            vT_ref[0, g, j] = vT[g * HEAD_DIM:(g + 1) * HEAD_DIM, j * TK:(j + 1) * TK]

    nat = _dot(xb, wnat_ref[...])
    o = D_POOL
    k_ref[0, 0] = nat[:, o:o + HEAD_DIM].astype(jnp.bfloat16)
    k_ref[0, 1] = nat[:, o + HEAD_DIM:o + KV_COLS].astype(jnp.bfloat16)
    ki_ref[0] = nat[:, o + KV_COLS:o + KV_COLS + IDX_DIM].astype(jnp.bfloat16)

    u = nat[:, :D_POOL]

    @pl.when(s_blk == 0)
    def _():
        hist_ref[...] = jnp.zeros_like(hist_ref)

    ext_ref[0:MAX_WIN, :] = hist_ref[...]
    ext_ref[MAX_WIN:MAX_WIN + tm, :] = u
    hist_ref[...] = u[tm - MAX_WIN:tm, :]

    pos = s_blk * tm + lax.broadcasted_iota(jnp.int32, (tm, POOL_GROUP_DIM), 0)
    for g, win in enumerate(POOL_WINDOWS):
        c0 = g * POOL_GROUP_DIM
        acc = ext_ref[MAX_WIN:MAX_WIN + tm, c0:c0 + POOL_GROUP_DIM]
        for j in range(1, win):
            acc = acc + ext_ref[MAX_WIN - j:MAX_WIN - j + tm, c0:c0 + POOL_GROUP_DIM]
        count = jnp.minimum(pos + 1, win).astype(jnp.float32)
        d = acc / count - ext_ref[MAX_WIN:MAX_WIN + tm, c0:c0 + POOL_GROUP_DIM]
        y = _dot(d.astype(jnp.bfloat16), wpool_ref[g]) * pscale_ref[:, c0:c0 + POOL_GROUP_DIM]
        pool_ref[0, :, c0:c0 + POOL_GROUP_DIM] = y.astype(jnp.bfloat16)


def _input_projection(x, w_in, w_pool, pool_scale):
    B, S, D = x.shape
    tm = TM_PROJ
    bf = jnp.bfloat16
    idx_scale = (IDX_DIM ** -0.5) * (IDX_HEADS ** -0.5)
    q_scale = (HEAD_DIM ** -0.5) * math.log2(math.e)

    o = 0
    w_q = w_in[:, o:o + Q_COLS]; o += Q_COLS
    w_k = w_in[:, o:o + KV_COLS]; o += KV_COLS
    w_v = w_in[:, o:o + KV_COLS]; o += KV_COLS
    w_pin = w_in[:, o:o + D_POOL]; o += D_POOL
    w_qi = w_in[:, o:o + QI_COLS]; o += QI_COLS
    w_ki = w_in[:, o:o + IDX_DIM]; o += IDX_DIM
    w_wi = w_in[:, o:o + IDX_HEADS]

    wqT = (w_q * q_scale).T.astype(bf)
    wvT = w_v.T.astype(bf)
    wqiT = w_qi.T.astype(bf)
    wwiT = jnp.pad(w_wi.T, ((0, 2 * SUBLANES - IDX_HEADS), (0, 0))).astype(bf)
    wnat = jnp.concatenate([w_pin, w_k, w_ki], axis=1).astype(bf)

    full = lambda a: pl.BlockSpec(a.shape, lambda b, s: (0,) * a.ndim)
    weights = (wqT, wvT, wqiT, wwiT, wnat, w_pool.astype(bf), pool_scale.reshape(1, D_POOL))
    out_shape = (
        jax.ShapeDtypeStruct((B, Q_COLS, S), bf),
        jax.ShapeDtypeStruct((B, N_KV_HEADS, S // TK, HEAD_DIM, TK), bf),
        jax.ShapeDtypeStruct((B, QI_COLS, S), bf),
        jax.ShapeDtypeStruct((B, IDX_HEADS, S), jnp.float32),
        jax.ShapeDtypeStruct((B, N_KV_HEADS, S, HEAD_DIM), bf),
        jax.ShapeDtypeStruct((B, S, IDX_DIM), bf),
        jax.ShapeDtypeStruct((B, S, D_POOL), bf),
    )
    out_specs = (
        pl.BlockSpec((1, Q_COLS, tm), lambda b, s: (b, 0, s)),
        pl.BlockSpec((1, N_KV_HEADS, tm // TK, HEAD_DIM, TK), lambda b, s: (b, 0, s, 0, 0)),
        pl.BlockSpec((1, QI_COLS, tm), lambda b, s: (b, 0, s)),
        pl.BlockSpec((1, IDX_HEADS, tm), lambda b, s: (b, 0, s)),
        pl.BlockSpec((1, N_KV_HEADS, tm, HEAD_DIM), lambda b, s: (b, 0, s, 0)),
        pl.BlockSpec((1, tm, IDX_DIM), lambda b, s: (b, s, 0)),
        pl.BlockSpec((1, tm, D_POOL), lambda b, s: (b, s, 0)),
    )
    return pl.pallas_call(
        functools.partial(_proj_kernel, idx_scale=idx_scale),
        grid=(B, S // tm),
        in_specs=[pl.BlockSpec((1, tm, D), lambda b, s: (b, s, 0))] + [full(w) for w in weights],
        out_specs=out_specs,
        out_shape=out_shape,
        scratch_shapes=[pltpu.VMEM((MAX_WIN, D_POOL), jnp.float32),
                        pltpu.VMEM((MAX_WIN + tm, D_POOL), jnp.float32)],
        compiler_params=pltpu.CompilerParams(
            dimension_semantics=("arbitrary", "arbitrary"), vmem_limit_bytes=VMEM_LIMIT_BYTES),
        name="dsa_input_projection",
    )(x, *weights)


def _dsa_kernel(qT_ref, qiT_ref, wiT_ref, k_ref, ki_ref, vT_ref, out_ref,
                scr_ref, s16_ref, s0_ref, sa_ref, sb_ref, m_ref, l_ref, acc_ref):
    q_blk = pl.program_id(1)
    n_chunks = q_blk + 1
    q0 = q_blk * TQ
    topk = float(TOPK_MAX)
    neg_inf = -jnp.inf

    row = lambda v: jnp.full((1, TQ), v, jnp.float32)
    flag = lambda cond: jnp.where(cond, 1.0, 0.0)
    qpos = q0 + lax.broadcasted_iota(jnp.int32, (1, TQ), 1)
    kiota = lax.broadcasted_iota(jnp.int32, (TK, TQ), 0)

    def load_chunk(c):
        return scr_ref[c]

    def store_chunk(c, val):
        scr_ref[c] = val

    N_ACC = 4

    def fold(x):
        parts = x.reshape(N_ACC, x.shape[0] // (N_ACC * SUBLANES), SUBLANES, TQ)
        return tuple(jnp.sum(parts[i], axis=0) for i in range(N_ACC))

    def add_folded(accs, x):
        return tuple(a + f for a, f in zip(accs, fold(x)))

    def finish(accs):
        return jnp.sum(functools.reduce(lambda a, b: a + b, accs), axis=0, keepdims=True)

    zero_accs = tuple(jnp.zeros((SUBLANES, TQ), jnp.float32) for _ in range(N_ACC))

    def score_piece(c, r0, carry, diagonal):
        mn, mx, a_ge, a_gt = carry
        rows = slice(r0, r0 + TK_PIECE)
        kic = ki_ref[0, c, rows, :]
        sc = jnp.zeros((TK_PIECE, TQ), jnp.float32)
        for h in range(IDX_HEADS):
            z = _dot(kic, qiT_ref[0, h * IDX_DIM:(h + 1) * IDX_DIM, :])
            sc = sc + jnp.maximum(z, 0.0) * wiT_ref[0, h:h + 1, :]
        sc_hi = sc
        if diagonal:
            valid = (q0 + r0 + lax.broadcasted_iota(jnp.int32, (TK_PIECE, TQ), 0)) <= qpos
            sc_hi = jnp.where(valid, sc, jnp.inf)
            sc = jnp.where(valid, sc, neg_inf)
        scr_ref[c, rows, :] = sc
        s16_ref[c, rows, :] = sc.astype(jnp.bfloat16)
        mx = jnp.maximum(mx, jnp.max(sc, axis=0, keepdims=True))
        mn = jnp.minimum(mn, jnp.min(sc_hi, axis=0, keepdims=True))
        return mn, mx, add_folded(a_ge, flag(sc >= 0.0)), add_folded(a_gt, flag(sc > 0.0))

    def score_chunk(c, carry, diagonal=False):
        for r0 in range(0, TK, TK_PIECE):
            carry = score_piece(c, r0, carry, diagonal)
        return carry

    def score_two(i, carry):
        return score_chunk(2 * i + 1, score_chunk(2 * i, carry))
    carry = lax.fori_loop(0, q_blk // 2, score_two, (row(jnp.inf), row(neg_inf), zero_accs, zero_accs))
    carry = lax.cond(q_blk % 2 == 1, lambda cr: score_chunk(q_blk - 1, cr), lambda cr: cr, carry)
    smin, smax, a_ge, a_gt = score_chunk(q_blk, carry, diagonal=True)
    c_ge0 = finish(a_ge)
    c_gt0 = finish(a_gt)

    def count_ge(p):
        def body(c, accs):
            return add_folded(accs, flag(load_chunk(c) >= p))
        return finish(lax.fori_loop(0, n_chunks, body, zero_accs))

    n_valid = (qpos + 1).astype(jnp.float32)
    short = flag(n_valid <= topk)
    at_zero = (1.0 - short) * flag(c_ge0 >= topk) * flag(c_gt0 <= topk)
    above = (1.0 - short) * flag(c_gt0 > topk)
    below = (1.0 - short) * flag(c_ge0 < topk)

    thr = jnp.where(short > 0.0, -F32_MAX, 0.0)
    keep = jnp.where(at_zero * flag(c_ge0 > topk) > 0.0, topk - c_gt0, KEEP_ALL)
    done = jnp.maximum(short, at_zero)
    lo = jnp.where(above > 0.0, 0.0, smin)
    hi = jnp.where(below > 0.0, 0.0, jnp.inf)
    c_hi = jnp.where(below > 0.0, c_ge0, 0.0)

    def coarse_step(_, st):
        lo, hi = st
        p16 = (0.5 * lo + 0.5 * jnp.minimum(hi, smax)).astype(jnp.bfloat16)
        p = p16.astype(jnp.float32)
        inside = flag(p > lo) * flag(p < hi)
        def body(c, accs):
            hit = jnp.where(s16_ref[c] >= p16, one16, zero16)
            parts = hit.reshape(N_ACC, TK // (N_ACC * PACKED_SUBLANES), PACKED_SUBLANES, TQ)
            return tuple(functools.reduce(lambda x, y: x + y, [a] + [parts[i, r] for r in range(parts.shape[1])])
                         for i, a in enumerate(accs))
        accs = lax.fori_loop(0, n_chunks, body, zero_accs16)
        c = finish(tuple(a.astype(jnp.float32) for a in accs))
        lo = jnp.where(inside * flag(c >= topk) > 0.0, p, lo)
        hi = jnp.where(inside * flag(c < topk) > 0.0, p, hi)
        return lo, hi

    one16 = jnp.ones((), jnp.bfloat16)
    zero16 = jnp.zeros((), jnp.bfloat16)
    zero_accs16 = tuple(jnp.zeros((PACKED_SUBLANES, TQ), jnp.bfloat16) for _ in range(N_ACC))
    lo, hi_coarse = lax.fori_loop(0, COARSE_ITERS, coarse_step, (lo, hi))
    lo = lo - (jnp.abs(lo) * BF16_ULP_BOUND + F32_TINY)

    def n_open(done, parked):
        return jnp.sum(flag(done + parked <= 0.0))

    def probe(p, active, lo, hi, c_hi, thr, done):
        c = count_ge(p)
        is_eq = active * flag(c == topk)
        is_gt = active * flag(c > topk)
        is_lt = active * flag(c < topk)
        thr = jnp.where(is_eq > 0.0, p, thr)
        done = jnp.maximum(done, is_eq)
        lo = jnp.where(is_gt > 0.0, p, lo)
        hi = jnp.where(is_lt > 0.0, p, hi)
        c_hi = jnp.where(is_lt > 0.0, c, c_hi)
        return lo, hi, c_hi, thr, done

    def bisect_step(lo, hi, c_hi, thr, done, parked):
        p = 0.5 * lo + 0.5 * jnp.minimum(hi, smax)
        parked = jnp.maximum(parked, 1.0 - flag(p > lo) * flag(p < hi))
        parked = jnp.maximum(parked, flag(c_hi == topk - 1.0))
        active = flag(done + parked <= 0.0)
        return probe(p, active, lo, hi, c_hi, thr, done) + (parked,)

    lo, hi, c_hi, thr, done = probe(jnp.minimum(hi_coarse, smax), 1.0 - done, lo, hi, c_hi, thr, done)
    vec = lax.fori_loop(0, FINE_ITERS_UNCHECKED, lambda _, v: bisect_step(*v),
                        (lo, hi, c_hi, thr, done, row(0.0)))

    def bisect_cond(st):
        return jnp.logical_and(st[0] < BISECT_MAX_ITERS, st[1] > 0.0)

    def bisect_body(st):
        vec = bisect_step(*bisect_step(*st[2:]))
        return (st[0] + 2, n_open(vec[4], vec[5])) + vec

    st = (jnp.int32(0), n_open(vec[4], vec[5])) + vec
    _, _, _, hi, c_hi, thr, done, _ = lax.while_loop(bisect_cond, bisect_body, st)

    def walk_cond(st):
        return st[0] > 0.0

    def walk_body(st):
        _, hi, c_hi, thr, keep, done = st
        def below_max(c, acc):
            blk = load_chunk(c)
            part = jnp.where(blk < hi, blk, neg_inf)
            return jnp.maximum(acc, jnp.max(part.reshape(TK // SUBLANES, SUBLANES, TQ), axis=0))
        m8 = lax.fori_loop(0, n_chunks, below_max, jnp.full((SUBLANES, TQ), neg_inf, jnp.float32))
        m = jnp.max(m8, axis=0, keepdims=True)
        c = count_ge(m)
        active = 1.0 - done
        reach = active * flag(c >= topk)
        thr = jnp.where(reach > 0.0, m, thr)
        keep = jnp.where(reach * flag(c > topk) > 0.0, topk - c_hi, keep)
        done = jnp.maximum(done, reach)
        step = active * (1.0 - reach)
        hi = jnp.where(step > 0.0, m, hi)
        c_hi = jnp.where(step > 0.0, c, c_hi)
        return (jnp.sum(1.0 - done), hi, c_hi, thr, keep, done)

    st = (jnp.sum(1.0 - done), hi, c_hi, thr, keep, done)
    _, _, _, thr, keep, _ = lax.while_loop(walk_cond, walk_body, st)

    has_ties = jnp.sum(flag(keep < KEEP_ALL)) > 0.0

    @pl.when(jnp.logical_not(has_ties))
    def _():
        def body(c, carry):
            store_chunk(c, jnp.where(load_chunk(c) >= thr, 0.0, neg_inf))
            return carry
        lax.fori_loop(0, n_chunks, body, 0)

    @pl.when(has_ties)
    def _():
        tri = flag(lax.broadcasted_iota(jnp.int32, (TK, TK), 1)
                   <= lax.broadcasted_iota(jnp.int32, (TK, TK), 0)).astype(jnp.bfloat16)
        def body(c, seen):
            blk = load_chunk(c)
            eq = flag(blk == thr)
            rank = seen + _dot(tri, eq.astype(jnp.bfloat16))
            sel = jnp.maximum(flag(blk > thr), eq * flag(rank <= keep))
            store_chunk(c, jnp.where(sel > 0.0, 0.0, neg_inf))
            return seen + finish(fold(eq))
        lax.fori_loop(0, n_chunks, body, row(0.0))

    m_ref[...] = jnp.full_like(m_ref, neg_inf)
    l_ref[...] = jnp.zeros_like(l_ref)
    acc_ref[...] = jnp.zeros_like(acc_ref)

    last = n_chunks - 1

    def stage_head(h, c, bias, dst_ref):
        dst_ref[h] = bias + _dot(k_ref[0, h // Q_PER_KV, c], qT_ref[0, h * HEAD_DIM:(h + 1) * HEAD_DIM, :])

    def softmax_head(h, c, src_ref):
        m_old = m_ref[h]
        m_new = jnp.maximum(m_old, jnp.max(src_ref[h], axis=0, keepdims=True))
        m_safe = jnp.where(m_new == neg_inf, 0.0, m_new)
        corr = jnp.exp2(m_old - m_safe)
        p = jnp.exp2(src_ref[h] - m_safe)
        l_ref[h] = l_ref[h] * corr + jnp.sum(p, axis=0, keepdims=True)
        acc_ref[h] = acc_ref[h] * corr + _dot(vT_ref[0, h // Q_PER_KV, c], p.astype(jnp.bfloat16))
        m_ref[h] = m_new

    def step(c, src_ref, c_next, dst_ref):
        bias_next = scr_ref[c_next]
        for h in range(STAGE_LEAD):
            stage_head(h, c_next, bias_next, dst_ref)
        for h in range(N_HEADS):
            softmax_head(h, c, src_ref)
            if h + STAGE_LEAD < N_HEADS:
                stage_head(h + STAGE_LEAD, c_next, bias_next, dst_ref)

    odd = n_chunks % 2

    @pl.when(odd == 1)
    def _():
        bias0 = scr_ref[0]
        for h in range(N_HEADS):
            stage_head(h, 0, bias0, s0_ref)
        step(0, s0_ref, jnp.minimum(1, last), sa_ref)

    @pl.when(odd == 0)
    def _():
        bias0 = scr_ref[0]
        for h in range(N_HEADS):
            stage_head(h, 0, bias0, sa_ref)

    def attend_pair(i, carry):
        c0 = odd + 2 * i
        step(c0, sa_ref, c0 + 1, sb_ref)
        step(c0 + 1, sb_ref, jnp.minimum(c0 + 2, last), sa_ref)
        return carry

    lax.fori_loop(0, n_chunks // 2, attend_pair, 0)
    for h in range(N_HEADS):
        acc_ref[h] = acc_ref[h] / l_ref[h]

    outT = acc_ref[...].reshape(N_HEADS * HEAD_DIM, TQ)
    out_ref[0] = outT.T.astype(jnp.bfloat16)


def _dsa_attention(qT, qiT, wiT, k, ki, vT):
    B, _, S = qT.shape
    k = k.reshape(B, N_KV_HEADS, S // TK, TK, HEAD_DIM)
    ki = ki.reshape(B, S // TK, TK, IDX_DIM)
    per_batch = lambda a: pl.BlockSpec((1,) + a.shape[1:], lambda b, i: (b,) + (0,) * (a.ndim - 1))
    return pl.pallas_call(
        _dsa_kernel,
        grid=(B, S // TQ),
        in_specs=[
            pl.BlockSpec((1, Q_COLS, TQ), lambda b, i: (b, 0, i)),
            pl.BlockSpec((1, QI_COLS, TQ), lambda b, i: (b, 0, i)),
            pl.BlockSpec((1, IDX_HEADS, TQ), lambda b, i: (b, 0, i)),
            per_batch(k), per_batch(ki), per_batch(vT),
        ],
        out_specs=pl.BlockSpec((1, TQ, Q_COLS), lambda b, i: (b, i, 0)),
        out_shape=jax.ShapeDtypeStruct((B, S, Q_COLS), jnp.bfloat16),
        scratch_shapes=[
            pltpu.VMEM((S // TK, TK, TQ), jnp.float32),
            pltpu.VMEM((S // TK, TK, TQ), jnp.bfloat16),
            pltpu.VMEM((N_HEADS, TK, TQ), jnp.float32),
            pltpu.VMEM((N_HEADS, TK, TQ), jnp.float32),
            pltpu.VMEM((N_HEADS, TK, TQ), jnp.float32),
            pltpu.VMEM((N_HEADS, 1, TQ), jnp.float32),
            pltpu.VMEM((N_HEADS, 1, TQ), jnp.float32),
            pltpu.VMEM((N_HEADS, HEAD_DIM, TQ), jnp.float32),
        ],
        compiler_params=pltpu.CompilerParams(
            dimension_semantics=("arbitrary", "arbitrary"), vmem_limit_bytes=VMEM_LIMIT_BYTES),
        name="dsa_attention",
    )(qT, qiT, wiT, k, ki, vT)


def _layer_norm(z, g, b):
    mu = jnp.mean(z, axis=-1, keepdims=True)
    zc = z - mu
    var = jnp.mean(zc * zc, axis=-1, keepdims=True)
    return zc * lax.rsqrt(var + LN_EPS) * g + b


def _router_gates(x1, hi, wr_ref, br_ref):
    lo = (x1 - hi.astype(jnp.float32)).astype(jnp.bfloat16)

    r = _dot(hi, wr_ref[...])
    logits = r[:, :LANES] + r[:, LANES:] + _dot(lo, wr_ref[:, :LANES]) + br_ref[...]

    tm = logits.shape[0]
    lane = lax.broadcasted_iota(jnp.int32, (tm, LANES), 1)
    neg_inf = -jnp.inf
    is_grp = jnp.where(lane >= N_EXPERTS, jnp.where(lane < N_EXPERTS + N_EXPERT_GROUPS, 1.0, 0.0), 0.0)
    gl = jnp.where(is_grp > 0.0, logits, neg_inf)
    g_max = jnp.max(gl, axis=1, keepdims=True)
    g_sel = jnp.min(jnp.where(gl == g_max, lane, LANES), axis=1, keepdims=True) - N_EXPERTS
    g_weight = 1.0 / jnp.sum(jnp.exp(gl - g_max), axis=1, keepdims=True)

    in_grp = jnp.right_shift(lane, EXPERTS_PER_GROUP.bit_length() - 1) == g_sel
    el = jnp.where(in_grp, logits, neg_inf)
    v1 = jnp.max(el, axis=1, keepdims=True)
    i1 = jnp.min(jnp.where(el == v1, lane, LANES), axis=1, keepdims=True)
    el2 = jnp.where(lane == i1, neg_inf, el)
    v2 = jnp.max(el2, axis=1, keepdims=True)
    i2 = jnp.min(jnp.where(el2 == v2, lane, LANES), axis=1, keepdims=True)
    e2 = jnp.exp(v2 - v1)
    w1 = g_weight / (1.0 + e2)
    w2 = g_weight * e2 / (1.0 + e2)
    return jnp.where(lane == i1, w1, jnp.where(lane == i2, w2, 0.0))


def _tail_kernel(attn_ref, pool_ref, x_ref, woa_ref, wop_ref, g1_ref, b1_ref, wr_ref, br_ref,
                 wg_ref, wu_ref, wd_ref, g2_ref, b2_ref, out_ref, hid_ref):
    mix = _dot(attn_ref[...], woa_ref[...]) + _dot(pool_ref[...], wop_ref[...])
    x1 = _layer_norm(ALPHA * x_ref[...] + mix, g1_ref[...], b1_ref[...])
    xb = x1.astype(jnp.bfloat16)
    gate = _router_gates(x1, xb, wr_ref, br_ref)

    lane = lax.broadcasted_iota(jnp.int32, gate.shape, 1)
    for e in range(N_EXPERTS):
        a = _dot(xb, wg_ref[e])
        b = _dot(xb, wu_ref[e])
        gcol = jnp.sum(jnp.where(lane == e, gate, 0.0), axis=1, keepdims=True)
        hid = (a * jax.nn.sigmoid(a)) * b * gcol
        hid_ref[:, e * D_EXPERT:(e + 1) * D_EXPERT] = hid.astype(jnp.bfloat16)
    y = _dot(hid_ref[...], wd_ref[...])
    out_ref[...] = _layer_norm(ALPHA * x1 + y, g2_ref[...], b2_ref[...])


def _output_and_experts(attn, pool, x2d, w_out, ln1_g, ln1_b, w_gr, b_gr, w_er, b_er,
                        w_gate, w_up, w_down, ln2_g, ln2_b):
    T, D = x2d.shape
    tm = TM_TAIL
    bf = jnp.bfloat16
    woa = w_out[:Q_COLS].astype(bf)
    wop = w_out[Q_COLS:].astype(bf)
    w_r = jnp.pad(jnp.concatenate([w_er, w_gr], axis=1), ((0, 0), (0, LANES - N_EXPERTS - N_EXPERT_GROUPS)))
    w_r_hi = w_r.astype(bf)
    w_r_lo = (w_r - w_r_hi.astype(jnp.float32)).astype(bf)
    wr = jnp.concatenate([w_r_hi, w_r_lo], axis=1)
    br = jnp.pad(jnp.concatenate([b_er, b_gr]), (0, LANES - N_EXPERTS - N_EXPERT_GROUPS)).reshape(1, LANES)
    wg = w_gate.reshape(N_EXPERTS, D, D_EXPERT).astype(bf)
    wu = w_up.reshape(N_EXPERTS, D, D_EXPERT).astype(bf)
    wd = w_down.reshape(N_EXPERTS * D_EXPERT, D).astype(bf)
    resident = lambda a: pl.BlockSpec(a.shape, lambda i: (0,) * a.ndim, pipeline_mode=pl.Buffered(1))
    consts = (woa, wop, ln1_g.reshape(1, D), ln1_b.reshape(1, D), wr, br,
              wg, wu, wd, ln2_g.reshape(1, D), ln2_b.reshape(1, D))
    return pl.pallas_call(
        _tail_kernel,
        grid=(T // tm,),
        in_specs=[pl.BlockSpec((tm, Q_COLS), lambda i: (i, 0)),
                  pl.BlockSpec((tm, D_POOL), lambda i: (i, 0)),
                  pl.BlockSpec((tm, D), lambda i: (i, 0))] + [resident(c) for c in consts],
        out_specs=pl.BlockSpec((tm, D), lambda i: (i, 0)),
        out_shape=jax.ShapeDtypeStruct((T, D), jnp.float32),
        scratch_shapes=[pltpu.VMEM((tm, N_EXPERTS * D_EXPERT), bf)],
        compiler_params=pltpu.CompilerParams(
            dimension_semantics=("arbitrary",), vmem_limit_bytes=VMEM_LIMIT_BYTES),
        name="dsa_output_and_experts",
    )(attn, pool, x2d, *consts)


def kernel(x, w_in, w_pool, pool_scale, w_out, ln1_g, ln1_b, w_group_router, b_group_router,
           w_expert_router, b_expert_router, w_gate, w_up, w_down, ln2_g, ln2_b):
    B, S, D = x.shape
    assert D == D_MODEL and S % TM_PROJ == 0 and S % TQ == 0 and TQ == TK
    assert w_in.shape[0] == DEPTH == 1
    l = 0
    qT, vT, qiT, wiT, k, ki, pool = _input_projection(x, w_in[l], w_pool[l], pool_scale[l])
    attn = _dsa_attention(qT, qiT, wiT, k, ki, vT)
    x2d = x.reshape(B * S, D)
    out = _output_and_experts(
        attn.reshape(B * S, Q_COLS), pool.reshape(B * S, D_POOL), x2d, w_out[l], ln1_g[l], ln1_b[l],
        w_group_router[l], b_group_router[l], w_expert_router[l], b_expert_router[l],
        w_gate[l], w_up[l], w_down[l], ln2_g[l], ln2_b[l])
    return out.reshape(B, S, D)
```

```python
import functools
import math

import jax
import jax.numpy as jnp
from jax import lax
from jax.experimental import pallas as pl
from jax.experimental.pallas import tpu as pltpu

D_MODEL = 1024
N_HEADS = 8
HEAD_DIM = 64
N_KV_HEADS = 2
Q_PER_KV = N_HEADS // N_KV_HEADS
IDX_HEADS = 8
IDX_DIM = 64
TOPK_MAX = 256
POOL_WINDOWS = (2, 4, 8, 16)
N_POOL_GROUPS = len(POOL_WINDOWS)
POOL_GROUP_DIM = 128
D_POOL = N_POOL_GROUPS * POOL_GROUP_DIM
MAX_WIN = max(POOL_WINDOWS)
Q_COLS = N_HEADS * HEAD_DIM
KV_COLS = N_KV_HEADS * HEAD_DIM
QI_COLS = IDX_HEADS * IDX_DIM
N_EXPERT_GROUPS = 4
EXPERTS_PER_GROUP = 4
N_EXPERTS = N_EXPERT_GROUPS * EXPERTS_PER_GROUP
D_EXPERT = 256
DEPTH = 1
ALPHA = float((2 * DEPTH) ** 0.25)
LN_EPS = 1e-5

LANES = 128
SUBLANES = 8
MXU_DIM = 256
VMEM_LIMIT_BYTES = 56 * 1024 * 1024

TM_PROJ = 512
TQ = MXU_DIM
TK = MXU_DIM
TK_PIECE = LANES
STAGE_LEAD = 2
TM_TAIL = 512
ROW_TILE = LANES
POS_RADIX = 16

F32_MAX = float(jnp.finfo(jnp.float32).max)
KEEP_ALL = 1.0e9
PACKED_SUBLANES = 2 * SUBLANES
BF16_ULP_BOUND = 2.0 ** -7
F32_TINY = 2.0 ** -120
COARSE_ITERS = 10
FINE_ITERS_UNCHECKED = 6
BISECT_MAX_ITERS = 40

_NT_DIMS = (((1,), (1,)), ((), ()))


def _dot(a, b):
    return jnp.dot(a, b, preferred_element_type=jnp.float32)


def _dot_nt(a, b):
    return lax.dot_general(a, b, _NT_DIMS, preferred_element_type=jnp.float32)


def _proj_kernel(x_ref, wqT_ref, wvT_ref, wqiT_ref, wwiT_ref, wnat_ref, wpool_ref, pscale_ref,
                 qT_ref, vT_ref, qiT_ref, wiT_ref, k_ref, ki_ref, pool_ref,
                 hist_ref, ext_ref, *, idx_scale):
    s_blk = pl.program_id(1)
    tm = x_ref.shape[1]
    xb = x_ref[0].astype(jnp.bfloat16)

    qT_ref[0] = _dot_nt(wqT_ref[...], xb).astype(jnp.bfloat16)
    qiT_ref[0] = _dot_nt(wqiT_ref[...], xb).astype(jnp.bfloat16)
    wiT_ref[0] = _dot_nt(wwiT_ref[...], xb)[:IDX_HEADS] * idx_scale
    vT = _dot_nt(wvT_ref[...], xb).astype(jnp.bfloat16)
    for g in range(N_KV_HEADS):
        for j in range(tm // TK):
            vT_ref[0, g, j] = vT[g * HEAD_DIM:(g + 1) * HEAD_DIM, j * TK:(j + 1) * TK]

    nat = _dot(xb, wnat_ref[...])
    o = D_POOL
    k_ref[0, 0] = nat[:, o:o + HEAD_DIM].astype(jnp.bfloat16)
    k_ref[0, 1] = nat[:, o + HEAD_DIM:o + KV_COLS].astype(jnp.bfloat16)
    ki_ref[0] = nat[:, o + KV_COLS:o + KV_COLS + IDX_DIM].astype(jnp.bfloat16)

    u = nat[:, :D_POOL]

    @pl.when(s_blk == 0)
    def _():
        hist_ref[...] = jnp.zeros_like(hist_ref)

    ext_ref[0:MAX_WIN, :] = hist_ref[...]
    ext_ref[MAX_WIN:MAX_WIN + tm, :] = u
    hist_ref[...] = u[tm - MAX_WIN:tm, :]

    pos = s_blk * tm + lax.broadcasted_iota(jnp.int32, (tm, POOL_GROUP_DIM), 0)
    for g, win in enumerate(POOL_WINDOWS):
        c0 = g * POOL_GROUP_DIM
        acc = ext_ref[MAX_WIN:MAX_WIN + tm, c0:c0 + POOL_GROUP_DIM]
        for j in range(1, win):
            acc = acc + ext_ref[MAX_WIN - j:MAX_WIN - j + tm, c0:c0 + POOL_GROUP_DIM]
        count = jnp.minimum(pos + 1, win).astype(jnp.float32)
        d = acc / count - ext_ref[MAX_WIN:MAX_WIN + tm, c0:c0 + POOL_GROUP_DIM]
        y = _dot(d.astype(jnp.bfloat16), wpool_ref[g]) * pscale_ref[:, c0:c0 + POOL_GROUP_DIM]
        pool_ref[0, :, c0:c0 + POOL_GROUP_DIM] = y.astype(jnp.bfloat16)


def _input_projection(x, w_in, w_pool, pool_scale):
    B, S, D = x.shape
    tm = TM_PROJ
    bf = jnp.bfloat16
    idx_scale = (IDX_DIM ** -0.5) * (IDX_HEADS ** -0.5)
    q_scale = (HEAD_DIM ** -0.5) * math.log2(math.e)

    o = 0
    w_q = w_in[:, o:o + Q_COLS]; o += Q_COLS
    w_k = w_in[:, o:o + KV_COLS]; o += KV_COLS
    w_v = w_in[:, o:o + KV_COLS]; o += KV_COLS
    w_pin = w_in[:, o:o + D_POOL]; o += D_POOL
    w_qi = w_in[:, o:o + QI_COLS]; o += QI_COLS
    w_ki = w_in[:, o:o + IDX_DIM]; o += IDX_DIM
    w_wi = w_in[:, o:o + IDX_HEADS]

    wqT = (w_q * q_scale).T.astype(bf)
    wvT = w_v.T.astype(bf)
    wqiT = w_qi.T.astype(bf)
    wwiT = jnp.pad(w_wi.T, ((0, 2 * SUBLANES - IDX_HEADS), (0, 0))).astype(bf)
    wnat = jnp.concatenate([w_pin, w_k, w_ki], axis=1).astype(bf)

    full = lambda a: pl.BlockSpec(a.shape, lambda b, s: (0,) * a.ndim)
    weights = (wqT, wvT, wqiT, wwiT, wnat, w_pool.astype(bf), pool_scale.reshape(1, D_POOL))
    out_shape = (
        jax.ShapeDtypeStruct((B, Q_COLS, S), bf),
        jax.ShapeDtypeStruct((B, N_KV_HEADS, S // TK, HEAD_DIM, TK), bf),
        jax.ShapeDtypeStruct((B, QI_COLS, S), bf),
        jax.ShapeDtypeStruct((B, IDX_HEADS, S), jnp.float32),
        jax.ShapeDtypeStruct((B, N_KV_HEADS, S, HEAD_DIM), bf),
        jax.ShapeDtypeStruct((B, S, IDX_DIM), bf),
        jax.ShapeDtypeStruct((B, S, D_POOL), bf),
    )
    out_specs = (
        pl.BlockSpec((1, Q_COLS, tm), lambda b, s: (b, 0, s)),
        pl.BlockSpec((1, N_KV_HEADS, tm // TK, HEAD_DIM, TK), lambda b, s: (b, 0, s, 0, 0)),
        pl.BlockSpec((1, QI_COLS, tm), lambda b, s: (b, 0, s)),
        pl.BlockSpec((1, IDX_HEADS, tm), lambda b, s: (b, 0, s)),
        pl.BlockSpec((1, N_KV_HEADS, tm, HEAD_DIM), lambda b, s: (b, 0, s, 0)),
        pl.BlockSpec((1, tm, IDX_DIM), lambda b, s: (b, s, 0)),
        pl.BlockSpec((1, tm, D_POOL), lambda b, s: (b, s, 0)),
    )
    return pl.pallas_call(
        functools.partial(_proj_kernel, idx_scale=idx_scale),
        grid=(B, S // tm),
        in_specs=[pl.BlockSpec((1, tm, D), lambda b, s: (b, s, 0))] + [full(w) for w in weights],
        out_specs=out_specs,
        out_shape=out_shape,
        scratch_shapes=[pltpu.VMEM((MAX_WIN, D_POOL), jnp.float32),
                        pltpu.VMEM((MAX_WIN + tm, D_POOL), jnp.float32)],
        compiler_params=pltpu.CompilerParams(
            dimension_semantics=("arbitrary", "arbitrary"), vmem_limit_bytes=VMEM_LIMIT_BYTES),
        name="dsa_input_projection",
    )(x, *weights)


def _dsa_kernel(qT_ref, qiT_ref, wiT_ref, k_ref, ki_ref, vT_ref, out_ref,
                scr_ref, s16_ref, s0_ref, sa_ref, sb_ref, m_ref, l_ref, acc_ref):
    q_blk = pl.program_id(1)
    n_chunks = q_blk + 1
    q0 = q_blk * TQ
    topk = float(TOPK_MAX)
    neg_inf = -jnp.inf

    row = lambda v: jnp.full((1, TQ), v, jnp.float32)
    flag = lambda cond: jnp.where(cond, 1.0, 0.0)
    qpos = q0 + lax.broadcasted_iota(jnp.int32, (1, TQ), 1)
    kiota = lax.broadcasted_iota(jnp.int32, (TK, TQ), 0)

    def load_chunk(c):
        return scr_ref[c]

    def store_chunk(c, val):
        scr_ref[c] = val

    N_ACC = 4

    def fold(x):
        parts = x.reshape(N_ACC, x.shape[0] // (N_ACC * SUBLANES), SUBLANES, TQ)
        return tuple(jnp.sum(parts[i], axis=0) for i in range(N_ACC))

    def add_folded(accs, x):
        return tuple(a + f for a, f in zip(accs, fold(x)))

    def finish(accs):
        return jnp.sum(functools.reduce(lambda a, b: a + b, accs), axis=0, keepdims=True)

    zero_accs = tuple(jnp.zeros((SUBLANES, TQ), jnp.float32) for _ in range(N_ACC))

    def score_piece(c, r0, carry, diagonal):
        mn, mx, a_ge, a_gt = carry
        rows = slice(r0, r0 + TK_PIECE)
        kic = ki_ref[0, c, rows, :]
        sc = jnp.zeros((TK_PIECE, TQ), jnp.float32)
        for h in range(IDX_HEADS):
            z = _dot(kic, qiT_ref[0, h * IDX_DIM:(h + 1) * IDX_DIM, :])
            sc = sc + jnp.maximum(z, 0.0) * wiT_ref[0, h:h + 1, :]
        sc_hi = sc
        if diagonal:
            valid = (q0 + r0 + lax.broadcasted_iota(jnp.int32, (TK_PIECE, TQ), 0)) <= qpos
            sc_hi = jnp.where(valid, sc, jnp.inf)
            sc = jnp.where(valid, sc, neg_inf)
        scr_ref[c, rows, :] = sc
        s16_ref[c, rows, :] = sc.astype(jnp.bfloat16)
        mx = jnp.maximum(mx, jnp.max(sc, axis=0, keepdims=True))
        mn = jnp.minimum(mn, jnp.min(sc_hi, axis=0, keepdims=True))
        return mn, mx, add_folded(a_ge, flag(sc >= 0.0)), add_folded(a_gt, flag(sc > 0.0))

    def score_chunk(c, carry, diagonal=False):
        for r0 in range(0, TK, TK_PIECE):
            carry = score_piece(c, r0, carry, diagonal)
        return carry

    def score_two(i, carry):
        return score_chunk(2 * i + 1, score_chunk(2 * i, carry))
    carry = lax.fori_loop(0, q_blk // 2, score_two, (row(jnp.inf), row(neg_inf), zero_accs, zero_accs))
    carry = lax.cond(q_blk % 2 == 1, lambda cr: score_chunk(q_blk - 1, cr), lambda cr: cr, carry)
    smin, smax, a_ge, a_gt = score_chunk(q_blk, carry, diagonal=True)
    c_ge0 = finish(a_ge)
    c_gt0 = finish(a_gt)

    def count_ge(p):
        def body(c, accs):
            return add_folded(accs, flag(load_chunk(c) >= p))
        return finish(lax.fori_loop(0, n_chunks, body, zero_accs))

    n_valid = (qpos + 1).astype(jnp.float32)
    short = flag(n_valid <= topk)
    at_zero = (1.0 - short) * flag(c_ge0 >= topk) * flag(c_gt0 <= topk)
    above = (1.0 - short) * flag(c_gt0 > topk)
    below = (1.0 - short) * flag(c_ge0 < topk)

    thr = jnp.where(short > 0.0, -F32_MAX, 0.0)
    keep = jnp.where(at_zero * flag(c_ge0 > topk) > 0.0, topk - c_gt0, KEEP_ALL)
    done = jnp.maximum(short, at_zero)
    lo = jnp.where(above > 0.0, 0.0, smin)
    hi = jnp.where(below > 0.0, 0.0, jnp.inf)
    c_hi = jnp.where(below > 0.0, c_ge0, 0.0)

    def coarse_step(_, st):
        lo, hi = st
        p16 = (0.5 * lo + 0.5 * jnp.minimum(hi, smax)).astype(jnp.bfloat16)
        p = p16.astype(jnp.float32)
        inside = flag(p > lo) * flag(p < hi)
        def body(c, accs):
            hit = jnp.where(s16_ref[c] >= p16, one16, zero16)
            parts = hit.reshape(N_ACC, TK // (N_ACC * PACKED_SUBLANES), PACKED_SUBLANES, TQ)
            return tuple(functools.reduce(lambda x, y: x + y, [a] + [parts[i, r] for r in range(parts.shape[1])])
                         for i, a in enumerate(accs))
        accs = lax.fori_loop(0, n_chunks, body, zero_accs16)
        c = finish(tuple(a.astype(jnp.float32) for a in accs))
        lo = jnp.where(inside * flag(c >= topk) > 0.0, p, lo)
        hi = jnp.where(inside * flag(c < topk) > 0.0, p, hi)
        return lo, hi

    one16 = jnp.ones((), jnp.bfloat16)
    zero16 = jnp.zeros((), jnp.bfloat16)
    zero_accs16 = tuple(jnp.zeros((PACKED_SUBLANES, TQ), jnp.bfloat16) for _ in range(N_ACC))
    lo, hi_coarse = lax.fori_loop(0, COARSE_ITERS, coarse_step, (lo, hi))
    lo = lo - (jnp.abs(lo) * BF16_ULP_BOUND + F32_TINY)

    def n_open(done, parked):
        return jnp.sum(flag(done + parked <= 0.0))

    def probe(p, active, lo, hi, c_hi, thr, done):
        c = count_ge(p)
        is_eq = active * flag(c == topk)
        is_gt = active * flag(c > topk)
        is_lt = active * flag(c < topk)
        thr = jnp.where(is_eq > 0.0, p, thr)
        done = jnp.maximum(done, is_eq)
        lo = jnp.where(is_gt > 0.0, p, lo)
        hi = jnp.where(is_lt > 0.0, p, hi)
        c_hi = jnp.where(is_lt > 0.0, c, c_hi)
        return lo, hi, c_hi, thr, done

    def bisect_step(lo, hi, c_hi, thr, done, parked):
        p = 0.5 * lo + 0.5 * jnp.minimum(hi, smax)
        parked = jnp.maximum(parked, 1.0 - flag(p > lo) * flag(p < hi))
        parked = jnp.maximum(parked, flag(c_hi == topk - 1.0))
        active = flag(done + parked <= 0.0)
        return probe(p, active, lo, hi, c_hi, thr, done) + (parked,)

    lo, hi, c_hi, thr, done = probe(jnp.minimum(hi_coarse, smax), 1.0 - done, lo, hi, c_hi, thr, done)
    vec = lax.fori_loop(0, FINE_ITERS_UNCHECKED, lambda _, v: bisect_step(*v),
                        (lo, hi, c_hi, thr, done, row(0.0)))

    def bisect_cond(st):
        return jnp.logical_and(st[0] < BISECT_MAX_ITERS, st[1] > 0.0)

    def bisect_body(st):
        vec = bisect_step(*bisect_step(*st[2:]))
        return (st[0] + 2, n_open(vec[4], vec[5])) + vec

    st = (jnp.int32(0), n_open(vec[4], vec[5])) + vec
    _, _, _, hi, c_hi, thr, done, _ = lax.while_loop(bisect_cond, bisect_body, st)

    def walk_cond(st):
        return st[0] > 0.0

    def walk_body(st):
        _, hi, c_hi, thr, keep, done = st
        def below_max(c, acc):
            blk = load_chunk(c)
            part = jnp.where(blk < hi, blk, neg_inf)
            return jnp.maximum(acc, jnp.max(part.reshape(TK // SUBLANES, SUBLANES, TQ), axis=0))
        m8 = lax.fori_loop(0, n_chunks, below_max, jnp.full((SUBLANES, TQ), neg_inf, jnp.float32))
        m = jnp.max(m8, axis=0, keepdims=True)
        c = count_ge(m)
        active = 1.0 - done
        reach = active * flag(c >= topk)
        thr = jnp.where(reach > 0.0, m, thr)
        keep = jnp.where(reach * flag(c > topk) > 0.0, topk - c_hi, keep)
        done = jnp.maximum(done, reach)
        step = active * (1.0 - reach)
        hi = jnp.where(step > 0.0, m, hi)
        c_hi = jnp.where(step > 0.0, c, c_hi)
        return (jnp.sum(1.0 - done), hi, c_hi, thr, keep, done)

    st = (jnp.sum(1.0 - done), hi, c_hi, thr, keep, done)
    _, _, _, thr, keep, _ = lax.while_loop(walk_cond, walk_body, st)

    has_ties = jnp.sum(flag(keep < KEEP_ALL)) > 0.0

    @pl.when(jnp.logical_not(has_ties))
    def _():
        def body(c, carry):
            store_chunk(c, jnp.where(load_chunk(c) >= thr, 0.0, neg_inf))
            return carry
        lax.fori_loop(0, n_chunks, body, 0)

    @pl.when(has_ties)
    def _():
        tri = flag(lax.broadcasted_iota(jnp.int32, (TK, TK), 1)
                   <= lax.broadcasted_iota(jnp.int32, (TK, TK), 0)).astype(jnp.bfloat16)
        def body(c, seen):
            blk = load_chunk(c)
            eq = flag(blk == thr)
            rank = seen + _dot(tri, eq.astype(jnp.bfloat16))
            sel = jnp.maximum(flag(blk > thr), eq * flag(rank <= keep))
            store_chunk(c, jnp.where(sel > 0.0, 0.0, neg_inf))
            return seen + finish(fold(eq))
        lax.fori_loop(0, n_chunks, body, row(0.0))

    m_ref[...] = jnp.full_like(m_ref, neg_inf)
    l_ref[...] = jnp.zeros_like(l_ref)
    acc_ref[...] = jnp.zeros_like(acc_ref)

    last = n_chunks - 1

    def stage_head(h, c, bias, dst_ref):
        dst_ref[h] = bias + _dot(k_ref[0, h // Q_PER_KV, c], qT_ref[0, h * HEAD_DIM:(h + 1) * HEAD_DIM, :])

    def softmax_head(h, c, src_ref):
        m_old = m_ref[h]
        m_new = jnp.maximum(m_old, jnp.max(src_ref[h], axis=0, keepdims=True))
        m_safe = jnp.where(m_new == neg_inf, 0.0, m_new)
        corr = jnp.exp2(m_old - m_safe)
        p = jnp.exp2(src_ref[h] - m_safe)
        l_ref[h] = l_ref[h] * corr + jnp.sum(p, axis=0, keepdims=True)
        acc_ref[h] = acc_ref[h] * corr + _dot(vT_ref[0, h // Q_PER_KV, c], p.astype(jnp.bfloat16))
        m_ref[h] = m_new

    def step(c, src_ref, c_next, dst_ref):
        bias_next = scr_ref[c_next]
        for h in range(STAGE_LEAD):
            stage_head(h, c_next, bias_next, dst_ref)
        for h in range(N_HEADS):
            softmax_head(h, c, src_ref)
            if h + STAGE_LEAD < N_HEADS:
                stage_head(h + STAGE_LEAD, c_next, bias_next, dst_ref)

    odd = n_chunks % 2

    @pl.when(odd == 1)
    def _():
        bias0 = scr_ref[0]
        for h in range(N_HEADS):
            stage_head(h, 0, bias0, s0_ref)
        step(0, s0_ref, jnp.minimum(1, last), sa_ref)

    @pl.when(odd == 0)
    def _():
        bias0 = scr_ref[0]
        for h in range(N_HEADS):
            stage_head(h, 0, bias0, sa_ref)

    def attend_pair(i, carry):
        c0 = odd + 2 * i
        step(c0, sa_ref, c0 + 1, sb_ref)
        step(c0 + 1, sb_ref, jnp.minimum(c0 + 2, last), sa_ref)
        return carry

    lax.fori_loop(0, n_chunks // 2, attend_pair, 0)
    for h in range(N_HEADS):
        acc_ref[h] = acc_ref[h] / l_ref[h]

    outT = acc_ref[...].reshape(N_HEADS * HEAD_DIM, TQ)
    out_ref[0] = outT.T.astype(jnp.bfloat16)


def _dsa_attention(qT, qiT, wiT, k, ki, vT):
    B, _, S = qT.shape
    k = k.reshape(B, N_KV_HEADS, S // TK, TK, HEAD_DIM)
    ki = ki.reshape(B, S // TK, TK, IDX_DIM)
    per_batch = lambda a: pl.BlockSpec((1,) + a.shape[1:], lambda b, i: (b,) + (0,) * (a.ndim - 1))
    return pl.pallas_call(
        _dsa_kernel,
        grid=(B, S // TQ),
        in_specs=[
            pl.BlockSpec((1, Q_COLS, TQ), lambda b, i: (b, 0, i)),
            pl.BlockSpec((1, QI_COLS, TQ), lambda b, i: (b, 0, i)),
            pl.BlockSpec((1, IDX_HEADS, TQ), lambda b, i: (b, 0, i)),
            per_batch(k), per_batch(ki), per_batch(vT),
        ],
        out_specs=pl.BlockSpec((1, TQ, Q_COLS), lambda b, i: (b, i, 0)),
        out_shape=jax.ShapeDtypeStruct((B, S, Q_COLS), jnp.bfloat16),
        scratch_shapes=[
            pltpu.VMEM((S // TK, TK, TQ), jnp.float32),
            pltpu.VMEM((S // TK, TK, TQ), jnp.bfloat16),
            pltpu.VMEM((N_HEADS, TK, TQ), jnp.float32),
            pltpu.VMEM((N_HEADS, TK, TQ), jnp.float32),
            pltpu.VMEM((N_HEADS, TK, TQ), jnp.float32),
            pltpu.VMEM((N_HEADS, 1, TQ), jnp.float32),
            pltpu.VMEM((N_HEADS, 1, TQ), jnp.float32),
            pltpu.VMEM((N_HEADS, HEAD_DIM, TQ), jnp.float32),
        ],
        compiler_params=pltpu.CompilerParams(
            dimension_semantics=("arbitrary", "arbitrary"), vmem_limit_bytes=VMEM_LIMIT_BYTES),
        name="dsa_attention",
    )(qT, qiT, wiT, k, ki, vT)


def _layer_norm(z, g, b):
    mu = jnp.mean(z, axis=-1, keepdims=True)
    zc = z - mu
    var = jnp.mean(zc * zc, axis=-1, keepdims=True)
    return zc * lax.rsqrt(var + LN_EPS) * g + b


def _router_gates(x1, hi, wr_ref, br_ref):
    lo = (x1 - hi.astype(jnp.float32)).astype(jnp.bfloat16)

    r = _dot(hi, wr_ref[...])
    logits = r[:, :LANES] + r[:, LANES:] + _dot(lo, wr_ref[:, :LANES]) + br_ref[...]

    tm = logits.shape[0]
    lane = lax.broadcasted_iota(jnp.int32, (tm, LANES), 1)
    neg_inf = -jnp.inf
    is_grp = jnp.where(lane >= N_EXPERTS, jnp.where(lane < N_EXPERTS + N_EXPERT_GROUPS, 1.0, 0.0), 0.0)
    gl = jnp.where(is_grp > 0.0, logits, neg_inf)
    g_max = jnp.max(gl, axis=1, keepdims=True)
    g_sel = jnp.min(jnp.where(gl == g_max, lane, LANES), axis=1, keepdims=True) - N_EXPERTS
    g_weight = 1.0 / jnp.sum(jnp.exp(gl - g_max), axis=1, keepdims=True)

    in_grp = jnp.right_shift(lane, EXPERTS_PER_GROUP.bit_length() - 1) == g_sel
    el = jnp.where(in_grp, logits, neg_inf)
    v1 = jnp.max(el, axis=1, keepdims=True)
    i1 = jnp.min(jnp.where(el == v1, lane, LANES), axis=1, keepdims=True)
    el2 = jnp.where(lane == i1, neg_inf, el)
    v2 = jnp.max(el2, axis=1, keepdims=True)
    i2 = jnp.min(jnp.where(el2 == v2, lane, LANES), axis=1, keepdims=True)
    e2 = jnp.exp(v2 - v1)
    w1 = g_weight / (1.0 + e2)
    w2 = g_weight * e2 / (1.0 + e2)
    return jnp.where(lane == i1, w1, jnp.where(lane == i2, w2, 0.0)), g_sel


def _split_bf16(v, parts):
    out = []
    for _ in range(parts):
        piece = v.astype(jnp.bfloat16)
        out.append(piece)
        v = v - piece.astype(jnp.float32)
    return out


def _tail_kernel(attn_ref, pool_ref, x_ref, woa_ref, wop_ref, g1_ref, b1_ref, wr_ref, br_ref,
                 wg_ref, wu_ref, wd_ref, g2_ref, b2_ref, out_ref, hid_ref, ys_ref):
    f32, bf = jnp.float32, jnp.bfloat16
    mix = _dot(attn_ref[...], woa_ref[...]) + _dot(pool_ref[...], wop_ref[...])
    x1 = _layer_norm(ALPHA * x_ref[...] + mix, g1_ref[...], b1_ref[...])
    xb = x1.astype(bf)
    gate, g_sel = _router_gates(x1, xb, wr_ref, br_ref)
    tm = gate.shape[0]
    flag = lambda cond: jnp.where(cond, 1.0, 0.0)

    lane = lax.broadcasted_iota(jnp.int32, (tm, LANES), 1)
    onehot = flag(lane == g_sel)
    counts = jnp.sum(onehot, axis=0, keepdims=True)
    earlier = flag(lax.broadcasted_iota(jnp.int32, (tm, tm), 1)
                   < lax.broadcasted_iota(jnp.int32, (tm, tm), 0)).astype(bf)
    before = _dot(earlier, onehot.astype(bf))
    rank = jnp.sum(onehot * before, axis=1, keepdims=True)
    start = jnp.sum(jnp.where(lane < g_sel, counts, 0.0), axis=1, keepdims=True)
    pos = start + rank

    pos_hi = jnp.floor(pos * (1.0 / POS_RADIX))
    digits = jnp.where(lane == 0, pos_hi, jnp.where(lane == 1, pos - POS_RADIX * pos_hi, 0.0)).astype(bf)
    pick = flag(lax.broadcasted_iota(jnp.int32, (PACKED_SUBLANES, LANES), 0)
                == lax.broadcasted_iota(jnp.int32, (PACKED_SUBLANES, LANES), 1)).astype(bf)
    digits_row = _dot_nt(pick, digits)
    pos_row = POS_RADIX * digits_row[0:1] + digits_row[1:2]

    lane1 = lax.broadcasted_iota(jnp.int32, (1, LANES), 1)
    group_count = [jnp.sum(jnp.where(lane1 == g, counts, 0.0)) for g in range(N_EXPERT_GROUPS)]
    group_start = [jnp.sum(jnp.where(lane1 < g, counts, 0.0)) for g in range(N_EXPERT_GROUPS)]
    gate_parts = _split_bf16(gate, 3)
    lane_rt = lax.broadcasted_iota(jnp.int32, (ROW_TILE, LANES), 1)

    ys_ref[...] = jnp.zeros_like(ys_ref)
    for r in range(tm // ROW_TILE):
        row0 = r * ROW_TILE
        tile_rows = slice(row0, row0 + ROW_TILE)
        sorted_row = (row0 + lax.broadcasted_iota(jnp.int32, (ROW_TILE, tm), 0)).astype(f32)
        perm = flag(sorted_row == pos_row).astype(bf)
        xs = _dot(perm, xb).astype(bf)
        gate_s = functools.reduce(lambda a, b: a + b, [_dot(perm, p) for p in gate_parts])
        for g in range(N_EXPERT_GROUPS):
            occurs = jnp.logical_and(
                group_count[g] > 0.0,
                jnp.logical_and(group_start[g] < row0 + ROW_TILE, group_start[g] + group_count[g] > row0))

            @pl.when(occurs)
            def _(g=g, xs=xs, gate_s=gate_s, tile_rows=tile_rows):
                for e in range(EXPERTS_PER_GROUP):
                    ex = g * EXPERTS_PER_GROUP + e
                    a = _dot(xs, wg_ref[ex])
                    b = _dot(xs, wu_ref[ex])
                    gcol = jnp.sum(jnp.where(lane_rt == ex, gate_s, 0.0), axis=1, keepdims=True)
                    hid = (a * jax.nn.sigmoid(a)) * b * gcol
                    hid_ref[:, e * D_EXPERT:(e + 1) * D_EXPERT] = hid.astype(bf)
                rows = slice(g * EXPERTS_PER_GROUP * D_EXPERT, (g + 1) * EXPERTS_PER_GROUP * D_EXPERT)
                ys_ref[tile_rows, :] += _dot(hid_ref[...], wd_ref[rows, :])

    unperm = flag(lax.broadcasted_iota(jnp.int32, (tm, tm), 1).astype(f32) == pos).astype(bf)
    y = functools.reduce(lambda a, b: a + b, [_dot(unperm, piece) for piece in _split_bf16(ys_ref[...], 2)])
    out_ref[...] = _layer_norm(ALPHA * x1 + y, g2_ref[...], b2_ref[...])


def _output_and_experts(attn, pool, x2d, w_out, ln1_g, ln1_b, w_gr, b_gr, w_er, b_er,
                        w_gate, w_up, w_down, ln2_g, ln2_b):
    T, D = x2d.shape
    tm = TM_TAIL
    bf = jnp.bfloat16
    woa = w_out[:Q_COLS].astype(bf)
    wop = w_out[Q_COLS:].astype(bf)
    w_r = jnp.pad(jnp.concatenate([w_er, w_gr], axis=1), ((0, 0), (0, LANES - N_EXPERTS - N_EXPERT_GROUPS)))
    w_r_hi = w_r.astype(bf)
    w_r_lo = (w_r - w_r_hi.astype(jnp.float32)).astype(bf)
    wr = jnp.concatenate([w_r_hi, w_r_lo], axis=1)
    br = jnp.pad(jnp.concatenate([b_er, b_gr]), (0, LANES - N_EXPERTS - N_EXPERT_GROUPS)).reshape(1, LANES)
    wg = w_gate.reshape(N_EXPERTS, D, D_EXPERT).astype(bf)
    wu = w_up.reshape(N_EXPERTS, D, D_EXPERT).astype(bf)
    wd = w_down.reshape(N_EXPERTS * D_EXPERT, D).astype(bf)
    resident = lambda a: pl.BlockSpec(a.shape, lambda i: (0,) * a.ndim, pipeline_mode=pl.Buffered(1))
    consts = (woa, wop, ln1_g.reshape(1, D), ln1_b.reshape(1, D), wr, br,
              wg, wu, wd, ln2_g.reshape(1, D), ln2_b.reshape(1, D))
    return pl.pallas_call(
        _tail_kernel,
        grid=(T // tm,),
        in_specs=[pl.BlockSpec((tm, Q_COLS), lambda i: (i, 0)),
                  pl.BlockSpec((tm, D_POOL), lambda i: (i, 0)),
                  pl.BlockSpec((tm, D), lambda i: (i, 0))] + [resident(c) for c in consts],
        out_specs=pl.BlockSpec((tm, D), lambda i: (i, 0)),
        out_shape=jax.ShapeDtypeStruct((T, D), jnp.float32),
        scratch_shapes=[pltpu.VMEM((ROW_TILE, EXPERTS_PER_GROUP * D_EXPERT), bf),
                        pltpu.VMEM((tm, D), jnp.float32)],
        compiler_params=pltpu.CompilerParams(
            dimension_semantics=("arbitrary",), vmem_limit_bytes=VMEM_LIMIT_BYTES),
        name="dsa_output_and_experts",
    )(attn, pool, x2d, *consts)


def kernel(x, w_in, w_pool, pool_scale, w_out, ln1_g, ln1_b, w_group_router, b_group_router,
           w_expert_router, b_expert_router, w_gate, w_up, w_down, ln2_g, ln2_b):
    B, S, D = x.shape
    assert D == D_MODEL and S % TM_PROJ == 0 and S % TQ == 0 and TQ == TK
    assert w_in.shape[0] == DEPTH == 1
    l = 0
    qT, vT, qiT, wiT, k, ki, pool = _input_projection(x, w_in[l], w_pool[l], pool_scale[l])
    attn = _dsa_attention(qT, qiT, wiT, k, ki, vT)
    x2d = x.reshape(B * S, D)
    out = _output_and_experts(
        attn.reshape(B * S, Q_COLS), pool.reshape(B * S, D_POOL), x2d, w_out[l], ln1_g[l], ln1_b[l],
        w_group_router[l], b_group_router[l], w_expert_router[l], b_expert_router[l],
        w_gate[l], w_up[l], w_down[l], ln2_g[l], ln2_b[l])
    return out.reshape(B, S, D)
```

```python
import functools
import math

import jax
import jax.numpy as jnp
from jax import lax
from jax.experimental import pallas as pl
from jax.experimental.pallas import tpu as pltpu

D_MODEL = 1024
N_HEADS = 8
HEAD_DIM = 64
N_KV_HEADS = 2
Q_PER_KV = N_HEADS // N_KV_HEADS
IDX_HEADS = 8
IDX_DIM = 64
TOPK_MAX = 256
POOL_WINDOWS = (2, 4, 8, 16)
N_POOL_GROUPS = len(POOL_WINDOWS)
POOL_GROUP_DIM = 128
D_POOL = N_POOL_GROUPS * POOL_GROUP_DIM
MAX_WIN = max(POOL_WINDOWS)
Q_COLS = N_HEADS * HEAD_DIM
KV_COLS = N_KV_HEADS * HEAD_DIM
QI_COLS = IDX_HEADS * IDX_DIM
N_EXPERT_GROUPS = 4
EXPERTS_PER_GROUP = 4
N_EXPERTS = N_EXPERT_GROUPS * EXPERTS_PER_GROUP
D_EXPERT = 256
DEPTH = 1
ALPHA = float((2 * DEPTH) ** 0.25)
LN_EPS = 1e-5

LANES = 128
SUBLANES = 8
MXU_DIM = 256
VMEM_LIMIT_BYTES = 56 * 1024 * 1024

TM_PROJ = 512
TQ = MXU_DIM
TK = MXU_DIM
TK_PIECE = LANES
STAGE_LEAD = 2
TM_TAIL = 512
ROW_TILE = LANES
POS_RADIX = 16

F32_MAX = float(jnp.finfo(jnp.float32).max)
KEEP_ALL = 1.0e9
PACKED_SUBLANES = 2 * SUBLANES
V_ROWS = HEAD_DIM + PACKED_SUBLANES
BF16_ULP_BOUND = 2.0 ** -7
F32_TINY = 2.0 ** -120
COARSE_ITERS = 10
FINE_ITERS_UNCHECKED = 6

_NT_DIMS = (((1,), (1,)), ((), ()))


def _dot(a, b):
    return jnp.dot(a, b, preferred_element_type=jnp.float32)


def _dot_nt(a, b):
    return lax.dot_general(a, b, _NT_DIMS, preferred_element_type=jnp.float32)


def _proj_kernel(x_ref, wqT_ref, wvT_ref, wqiT_ref, wwiT_ref, wnat_ref, wpool_ref, pscale_ref,
                 qT_ref, vT_ref, qiT_ref, wiT_ref, k_ref, ki_ref, pool_ref,
                 hist_ref, ext_ref, *, idx_scale):
    s_blk = pl.program_id(1)
    tm = x_ref.shape[1]
    xb = x_ref[0].astype(jnp.bfloat16)

    qT_ref[0] = _dot_nt(wqT_ref[...], xb).astype(jnp.bfloat16)
    qiT_ref[0] = _dot_nt(wqiT_ref[...], xb).astype(jnp.bfloat16)
    wiT_ref[0] = _dot_nt(wwiT_ref[...], xb)[:IDX_HEADS] * idx_scale
    vT = _dot_nt(wvT_ref[...], xb).astype(jnp.bfloat16)
    for g in range(N_KV_HEADS):
        for j in range(tm // TK):
            vT_ref[0, g, j, :HEAD_DIM, :] = vT[g * HEAD_DIM:(g + 1) * HEAD_DIM, j * TK:(j + 1) * TK]
            vT_ref[0, g, j, HEAD_DIM:, :] = jnp.ones((V_ROWS - HEAD_DIM, TK), jnp.bfloat16)

    nat = _dot(xb, wnat_ref[...])
    o = D_POOL
    k_ref[0, 0] = nat[:, o:o + HEAD_DIM].astype(jnp.bfloat16)
    k_ref[0, 1] = nat[:, o + HEAD_DIM:o + KV_COLS].astype(jnp.bfloat16)
    ki_ref[0] = nat[:, o + KV_COLS:o + KV_COLS + IDX_DIM].astype(jnp.bfloat16)

    u = nat[:, :D_POOL]

    @pl.when(s_blk == 0)
    def _():
        hist_ref[...] = jnp.zeros_like(hist_ref)

    ext_ref[0:MAX_WIN, :] = hist_ref[...]
    ext_ref[MAX_WIN:MAX_WIN + tm, :] = u
    hist_ref[...] = u[tm - MAX_WIN:tm, :]

    pos = s_blk * tm + lax.broadcasted_iota(jnp.int32, (tm, POOL_GROUP_DIM), 0)
    for g, win in enumerate(POOL_WINDOWS):
        c0 = g * POOL_GROUP_DIM
        acc = ext_ref[MAX_WIN:MAX_WIN + tm, c0:c0 + POOL_GROUP_DIM]
        for j in range(1, win):
            acc = acc + ext_ref[MAX_WIN - j:MAX_WIN - j + tm, c0:c0 + POOL_GROUP_DIM]
        count = jnp.minimum(pos + 1, win).astype(jnp.float32)
        d = acc / count - ext_ref[MAX_WIN:MAX_WIN + tm, c0:c0 + POOL_GROUP_DIM]
        y = _dot(d.astype(jnp.bfloat16), wpool_ref[g]) * pscale_ref[:, c0:c0 + POOL_GROUP_DIM]
        pool_ref[0, :, c0:c0 + POOL_GROUP_DIM] = y.astype(jnp.bfloat16)


def _input_projection(x, w_in, w_pool, pool_scale):
    B, S, D = x.shape
    tm = TM_PROJ
    bf = jnp.bfloat16
    idx_scale = (IDX_DIM ** -0.5) * (IDX_HEADS ** -0.5)
    q_scale = (HEAD_DIM ** -0.5) * math.log2(math.e)

    o = 0
    w_q = w_in[:, o:o + Q_COLS]; o += Q_COLS
    w_k = w_in[:, o:o + KV_COLS]; o += KV_COLS
    w_v = w_in[:, o:o + KV_COLS]; o += KV_COLS
    w_pin = w_in[:, o:o + D_POOL]; o += D_POOL
    w_qi = w_in[:, o:o + QI_COLS]; o += QI_COLS
    w_ki = w_in[:, o:o + IDX_DIM]; o += IDX_DIM
    w_wi = w_in[:, o:o + IDX_HEADS]

    wqT = (w_q * q_scale).T.astype(bf)
    wvT = w_v.T.astype(bf)
    wqiT = w_qi.T.astype(bf)
    wwiT = jnp.pad(w_wi.T, ((0, 2 * SUBLANES - IDX_HEADS), (0, 0))).astype(bf)
    wnat = jnp.concatenate([w_pin, w_k, w_ki], axis=1).astype(bf)

    full = lambda a: pl.BlockSpec(a.shape, lambda b, s: (0,) * a.ndim)
    weights = (wqT, wvT, wqiT, wwiT, wnat, w_pool.astype(bf), pool_scale.reshape(1, D_POOL))
    out_shape = (
        jax.ShapeDtypeStruct((B, Q_COLS, S), bf),
        jax.ShapeDtypeStruct((B, N_KV_HEADS, S // TK, V_ROWS, TK), bf),
        jax.ShapeDtypeStruct((B, QI_COLS, S), bf),
        jax.ShapeDtypeStruct((B, IDX_HEADS, S), jnp.float32),
        jax.ShapeDtypeStruct((B, N_KV_HEADS, S, HEAD_DIM), bf),
        jax.ShapeDtypeStruct((B, S, IDX_DIM), bf),
        jax.ShapeDtypeStruct((B, S, D_POOL), bf),
    )
    out_specs = (
        pl.BlockSpec((1, Q_COLS, tm), lambda b, s: (b, 0, s)),
        pl.BlockSpec((1, N_KV_HEADS, tm // TK, V_ROWS, TK), lambda b, s: (b, 0, s, 0, 0)),
        pl.BlockSpec((1, QI_COLS, tm), lambda b, s: (b, 0, s)),
        pl.BlockSpec((1, IDX_HEADS, tm), lambda b, s: (b, 0, s)),
        pl.BlockSpec((1, N_KV_HEADS, tm, HEAD_DIM), lambda b, s: (b, 0, s, 0)),
        pl.BlockSpec((1, tm, IDX_DIM), lambda b, s: (b, s, 0)),
        pl.BlockSpec((1, tm, D_POOL), lambda b, s: (b, s, 0)),
    )
    return pl.pallas_call(
        functools.partial(_proj_kernel, idx_scale=idx_scale),
        grid=(B, S // tm),
        in_specs=[pl.BlockSpec((1, tm, D), lambda b, s: (b, s, 0))] + [full(w) for w in weights],
        out_specs=out_specs,
        out_shape=out_shape,
        scratch_shapes=[pltpu.VMEM((MAX_WIN, D_POOL), jnp.float32),
                        pltpu.VMEM((MAX_WIN + tm, D_POOL), jnp.float32)],
        compiler_params=pltpu.CompilerParams(
            dimension_semantics=("arbitrary", "arbitrary"), vmem_limit_bytes=VMEM_LIMIT_BYTES),
        name="dsa_input_projection",
    )(x, *weights)


def _dsa_kernel(qT_ref, qiT_ref, wiT_ref, k_ref, ki_ref, vT_ref, out_ref,
                scr_ref, s16_ref, s0_ref, sa_ref, sb_ref, m_ref, acc_ref):
    q_blk = pl.program_id(1)
    n_chunks = q_blk + 1
    q0 = q_blk * TQ
    topk = float(TOPK_MAX)
    neg_inf = -jnp.inf

    row = lambda v: jnp.full((1, TQ), v, jnp.float32)
    flag = lambda cond: jnp.where(cond, 1.0, 0.0)
    qpos = q0 + lax.broadcasted_iota(jnp.int32, (1, TQ), 1)
    kiota = lax.broadcasted_iota(jnp.int32, (TK, TQ), 0)

    def load_chunk(c):
        return scr_ref[c]

    def store_chunk(c, val):
        scr_ref[c] = val

    N_ACC = 4

    def fold(x):
        parts = x.reshape(N_ACC, x.shape[0] // (N_ACC * SUBLANES), SUBLANES, TQ)
        return tuple(jnp.sum(parts[i], axis=0) for i in range(N_ACC))

    def add_folded(accs, x):
        return tuple(a + f for a, f in zip(accs, fold(x)))

    def finish(accs):
        return jnp.sum(functools.reduce(lambda a, b: a + b, accs), axis=0, keepdims=True)

    zero_accs = tuple(jnp.zeros((SUBLANES, TQ), jnp.float32) for _ in range(N_ACC))

    def score_piece(c, r0, carry, diagonal):
        mn, mx, a_ge, a_gt = carry
        rows = slice(r0, r0 + TK_PIECE)
        kic = ki_ref[0, c, rows, :]
        sc = jnp.zeros((TK_PIECE, TQ), jnp.float32)
        for h in range(IDX_HEADS):
            z = _dot(kic, qiT_ref[0, h * IDX_DIM:(h + 1) * IDX_DIM, :])
            sc = sc + jnp.maximum(z, 0.0) * wiT_ref[0, h:h + 1, :]
        sc_hi = sc
        if diagonal:
            valid = (q0 + r0 + lax.broadcasted_iota(jnp.int32, (TK_PIECE, TQ), 0)) <= qpos
            sc_hi = jnp.where(valid, sc, jnp.inf)
            sc = jnp.where(valid, sc, neg_inf)
        scr_ref[c, rows, :] = sc
        s16_ref[c, rows, :] = sc.astype(jnp.bfloat16)
        mx = jnp.maximum(mx, jnp.max(sc, axis=0, keepdims=True))
        mn = jnp.minimum(mn, jnp.min(sc_hi, axis=0, keepdims=True))
        return mn, mx, add_folded(a_ge, flag(sc >= 0.0)), add_folded(a_gt, flag(sc > 0.0))

    def score_chunk(c, carry, diagonal=False):
        for r0 in range(0, TK, TK_PIECE):
            carry = score_piece(c, r0, carry, diagonal)
        return carry

    def score_two(i, carry):
        return score_chunk(2 * i + 1, score_chunk(2 * i, carry))
    carry = lax.fori_loop(0, q_blk // 2, score_two, (row(jnp.inf), row(neg_inf), zero_accs, zero_accs))
    carry = lax.cond(q_blk % 2 == 1, lambda cr: score_chunk(q_blk - 1, cr), lambda cr: cr, carry)
    smin, smax, a_ge, a_gt = score_chunk(q_blk, carry, diagonal=True)
    c_ge0 = finish(a_ge)
    c_gt0 = finish(a_gt)

    def count_ge(p):
        def body(c, accs):
            return add_folded(accs, flag(load_chunk(c) >= p))
        return finish(lax.fori_loop(0, n_chunks, body, zero_accs))

    n_valid = (qpos + 1).astype(jnp.float32)
    short = flag(n_valid <= topk)
    at_zero = (1.0 - short) * flag(c_ge0 >= topk) * flag(c_gt0 <= topk)
    above = (1.0 - short) * flag(c_gt0 > topk)
    below = (1.0 - short) * flag(c_ge0 < topk)

    thr = jnp.where(short > 0.0, -F32_MAX, 0.0)
    keep = jnp.where(at_zero * flag(c_ge0 > topk) > 0.0, topk - c_gt0, KEEP_ALL)
    done = jnp.maximum(short, at_zero)
    lo = jnp.where(above > 0.0, 0.0, smin)
    hi = jnp.where(below > 0.0, 0.0, jnp.inf)
    c_hi = jnp.where(below > 0.0, c_ge0, 0.0)

    def coarse_step(_, st):
        lo, hi = st
        p16 = (0.5 * lo + 0.5 * jnp.minimum(hi, smax)).astype(jnp.bfloat16)
        p = p16.astype(jnp.float32)
        inside = flag(p > lo) * flag(p < hi)
        def body(c, accs):
            hit = jnp.where(s16_ref[c] >= p16, one16, zero16)
            parts = hit.reshape(N_ACC, TK // (N_ACC * PACKED_SUBLANES), PACKED_SUBLANES, TQ)
            return tuple(functools.reduce(lambda x, y: x + y, [a] + [parts[i, r] for r in range(parts.shape[1])])
                         for i, a in enumerate(accs))
        accs = lax.fori_loop(0, n_chunks, body, zero_accs16)
        c = finish(tuple(a.astype(jnp.float32) for a in accs))
        lo = jnp.where(inside * flag(c >= topk) > 0.0, p, lo)
        hi = jnp.where(inside * flag(c < topk) > 0.0, p, hi)
        return lo, hi

    one16 = jnp.ones((), jnp.bfloat16)
    zero16 = jnp.zeros((), jnp.bfloat16)
    zero_accs16 = tuple(jnp.zeros((PACKED_SUBLANES, TQ), jnp.bfloat16) for _ in range(N_ACC))
    lo, hi_coarse = lax.fori_loop(0, COARSE_ITERS, coarse_step, (lo, hi))
    lo = lo - (jnp.abs(lo) * BF16_ULP_BOUND + F32_TINY)

    def probe(p, active, lo, hi, c_hi, thr, done):
        c = count_ge(p)
        is_eq = active * flag(c == topk)
        is_gt = active * flag(c > topk)
        is_lt = active * flag(c < topk)
        thr = jnp.where(is_eq > 0.0, p, thr)
        done = jnp.maximum(done, is_eq)
        lo = jnp.where(is_gt > 0.0, p, lo)
        hi = jnp.where(is_lt > 0.0, p, hi)
        c_hi = jnp.where(is_lt > 0.0, c, c_hi)
        return lo, hi, c_hi, thr, done

    def bisect_step(lo, hi, c_hi, thr, done, parked):
        p = 0.5 * lo + 0.5 * jnp.minimum(hi, smax)
        parked = jnp.maximum(parked, 1.0 - flag(p > lo) * flag(p < hi))
        parked = jnp.maximum(parked, flag(c_hi == topk - 1.0))
        active = flag(done + parked <= 0.0)
        return probe(p, active, lo, hi, c_hi, thr, done) + (parked,)

    lo, hi, c_hi, thr, done = probe(jnp.minimum(hi_coarse, smax), 1.0 - done, lo, hi, c_hi, thr, done)
    lo, hi, c_hi, thr, done, _ = lax.fori_loop(
        0, FINE_ITERS_UNCHECKED, lambda _, v: bisect_step(*v), (lo, hi, c_hi, thr, done, row(0.0)))

    def walk_step(hi, c_hi, thr, keep, done):
        def below_max(c, acc):
            blk = load_chunk(c)
            part = jnp.where(blk < hi, blk, neg_inf)
            return jnp.maximum(acc, jnp.max(part.reshape(TK // SUBLANES, SUBLANES, TQ), axis=0))
        m8 = lax.fori_loop(0, n_chunks, below_max, jnp.full((SUBLANES, TQ), neg_inf, jnp.float32))
        m = jnp.max(m8, axis=0, keepdims=True)
        c = count_ge(m)
        active = 1.0 - done
        reach = active * flag(c >= topk)
        thr = jnp.where(reach > 0.0, m, thr)
        keep = jnp.where(reach * flag(c > topk) > 0.0, topk - c_hi, keep)
        done = jnp.maximum(done, reach)
        step = active * (1.0 - reach)
        hi = jnp.where(step > 0.0, m, hi)
        c_hi = jnp.where(step > 0.0, c, c_hi)
        return hi, c_hi, thr, keep, done

    hi, c_hi, thr, keep, done = walk_step(hi, c_hi, thr, keep, done)

    def leftover_cond(st):
        return st[0] > 0.0

    def leftover_body(st):
        _, lo, hi, c_hi, thr, keep, done = st
        lo, hi, c_hi, thr, done, _ = bisect_step(*bisect_step(lo, hi, c_hi, thr, done, row(0.0)))
        hi, c_hi, thr, keep, done = walk_step(hi, c_hi, thr, keep, done)
        return (jnp.sum(1.0 - done), lo, hi, c_hi, thr, keep, done)

    st = (jnp.sum(1.0 - done), lo, hi, c_hi, thr, keep, done)
    _, _, _, _, thr, keep, _ = lax.while_loop(leftover_cond, leftover_body, st)

    has_ties = jnp.sum(flag(keep < KEEP_ALL)) > 0.0

    @pl.when(jnp.logical_not(has_ties))
    def _():
        def body(c, carry):
            store_chunk(c, jnp.where(load_chunk(c) >= thr, 0.0, neg_inf))
            return carry
        lax.fori_loop(0, n_chunks, body, 0)

    @pl.when(has_ties)
    def _():
        tri = flag(lax.broadcasted_iota(jnp.int32, (TK, TK), 1)
                   <= lax.broadcasted_iota(jnp.int32, (TK, TK), 0)).astype(jnp.bfloat16)
        def body(c, seen):
            blk = load_chunk(c)
            eq = flag(blk == thr)
            rank = seen + _dot(tri, eq.astype(jnp.bfloat16))
            sel = jnp.maximum(flag(blk > thr), eq * flag(rank <= keep))
            store_chunk(c, jnp.where(sel > 0.0, 0.0, neg_inf))
            return seen + finish(fold(eq))
        lax.fori_loop(0, n_chunks, body, row(0.0))

    m_ref[...] = jnp.full_like(m_ref, neg_inf)
    acc_ref[...] = jnp.zeros_like(acc_ref)

    last = n_chunks - 1

    def stage_head(h, c, bias, dst_ref):
        dst_ref[h] = bias + _dot(k_ref[0, h // Q_PER_KV, c], qT_ref[0, h * HEAD_DIM:(h + 1) * HEAD_DIM, :])

    def softmax_head(h, c, src_ref):
        m_old = m_ref[h]
        m_new = jnp.maximum(m_old, jnp.max(src_ref[h], axis=0, keepdims=True))
        m_safe = jnp.where(m_new == neg_inf, 0.0, m_new)
        corr = jnp.exp2(m_old - m_safe)
        p = jnp.exp2(src_ref[h] - m_safe)
        acc_ref[h] = acc_ref[h] * corr + _dot(vT_ref[0, h // Q_PER_KV, c], p.astype(jnp.bfloat16))
        m_ref[h] = m_new

    def step(c, src_ref, c_next, dst_ref):
        bias_next = scr_ref[c_next]
        for h in range(STAGE_LEAD):
            stage_head(h, c_next, bias_next, dst_ref)
        for h in range(N_HEADS):
            softmax_head(h, c, src_ref)
            if h + STAGE_LEAD < N_HEADS:
                stage_head(h + STAGE_LEAD, c_next, bias_next, dst_ref)

    odd = n_chunks % 2

    @pl.when(odd == 1)
    def _():
        bias0 = scr_ref[0]
        for h in range(N_HEADS):
            stage_head(h, 0, bias0, s0_ref)
        step(0, s0_ref, jnp.minimum(1, last), sa_ref)

    @pl.when(odd == 0)
    def _():
        bias0 = scr_ref[0]
        for h in range(N_HEADS):
            stage_head(h, 0, bias0, sa_ref)

    def attend_pair(i, carry):
        c0 = odd + 2 * i
        step(c0, sa_ref, c0 + 1, sb_ref)
        step(c0 + 1, sb_ref, jnp.minimum(c0 + 2, last), sa_ref)
        return carry

    lax.fori_loop(0, n_chunks // 2, attend_pair, 0)
    outT = jnp.concatenate(
        [acc_ref[h, :HEAD_DIM, :] / acc_ref[h, HEAD_DIM:HEAD_DIM + 1, :] for h in range(N_HEADS)], axis=0)
    out_ref[0] = outT.T.astype(jnp.bfloat16)


def _dsa_attention(qT, qiT, wiT, k, ki, vT):
    B, _, S = qT.shape
    k = k.reshape(B, N_KV_HEADS, S // TK, TK, HEAD_DIM)
    ki = ki.reshape(B, S // TK, TK, IDX_DIM)
    per_batch = lambda a: pl.BlockSpec((1,) + a.shape[1:], lambda b, i: (b,) + (0,) * (a.ndim - 1))
    return pl.pallas_call(
        _dsa_kernel,
        grid=(B, S // TQ),
        in_specs=[
            pl.BlockSpec((1, Q_COLS, TQ), lambda b, i: (b, 0, i)),
            pl.BlockSpec((1, QI_COLS, TQ), lambda b, i: (b, 0, i)),
            pl.BlockSpec((1, IDX_HEADS, TQ), lambda b, i: (b, 0, i)),
            per_batch(k), per_batch(ki), per_batch(vT),
        ],
        out_specs=pl.BlockSpec((1, TQ, Q_COLS), lambda b, i: (b, i, 0)),
        out_shape=jax.ShapeDtypeStruct((B, S, Q_COLS), jnp.bfloat16),
        scratch_shapes=[
            pltpu.VMEM((S // TK, TK, TQ), jnp.float32),
            pltpu.VMEM((S // TK, TK, TQ), jnp.bfloat16),
            pltpu.VMEM((N_HEADS, TK, TQ), jnp.float32),
            pltpu.VMEM((N_HEADS, TK, TQ), jnp.float32),
            pltpu.VMEM((N_HEADS, TK, TQ), jnp.float32),
            pltpu.VMEM((N_HEADS, 1, TQ), jnp.float32),
            pltpu.VMEM((N_HEADS, V_ROWS, TQ), jnp.float32),
        ],
        compiler_params=pltpu.CompilerParams(
            dimension_semantics=("arbitrary", "arbitrary"), vmem_limit_bytes=VMEM_LIMIT_BYTES),
        name="dsa_attention",
    )(qT, qiT, wiT, k, ki, vT)


def _layer_norm(z, g, b):
    mu = jnp.mean(z, axis=-1, keepdims=True)
    zc = z - mu
    var = jnp.mean(zc * zc, axis=-1, keepdims=True)
    return zc * lax.rsqrt(var + LN_EPS) * g + b


def _router_gates(x1, hi, wr_ref, br_ref):
    lo = (x1 - hi.astype(jnp.float32)).astype(jnp.bfloat16)

    r = _dot(hi, wr_ref[...])
    logits = r[:, :LANES] + r[:, LANES:] + _dot(lo, wr_ref[:, :LANES]) + br_ref[...]

    tm = logits.shape[0]
    lane = lax.broadcasted_iota(jnp.int32, (tm, LANES), 1)
    neg_inf = -jnp.inf
    is_grp = jnp.where(lane >= N_EXPERTS, jnp.where(lane < N_EXPERTS + N_EXPERT_GROUPS, 1.0, 0.0), 0.0)
    gl = jnp.where(is_grp > 0.0, logits, neg_inf)
    g_max = jnp.max(gl, axis=1, keepdims=True)
    g_sel = jnp.min(jnp.where(gl == g_max, lane, LANES), axis=1, keepdims=True) - N_EXPERTS
    g_weight = 1.0 / jnp.sum(jnp.exp(gl - g_max), axis=1, keepdims=True)

    in_grp = jnp.right_shift(lane, EXPERTS_PER_GROUP.bit_length() - 1) == g_sel
    el = jnp.where(in_grp, logits, neg_inf)
    v1 = jnp.max(el, axis=1, keepdims=True)
    i1 = jnp.min(jnp.where(el == v1, lane, LANES), axis=1, keepdims=True)
    el2 = jnp.where(lane == i1, neg_inf, el)
    v2 = jnp.max(el2, axis=1, keepdims=True)
    i2 = jnp.min(jnp.where(el2 == v2, lane, LANES), axis=1, keepdims=True)
    e2 = jnp.exp(v2 - v1)
    w1 = g_weight / (1.0 + e2)
    w2 = g_weight * e2 / (1.0 + e2)
    return jnp.where(lane == i1, w1, jnp.where(lane == i2, w2, 0.0)), g_sel


def _split_bf16(v, parts):
    out = []
    for _ in range(parts):
        piece = v.astype(jnp.bfloat16)
        out.append(piece)
        v = v - piece.astype(jnp.float32)
    return out


def _tail_kernel(attn_ref, pool_ref, x_ref, woa_ref, wop_ref, g1_ref, b1_ref, wr_ref, br_ref,
                 wg_ref, wu_ref, wd_ref, g2_ref, b2_ref, out_ref, hid_ref, ys_ref):
    f32, bf = jnp.float32, jnp.bfloat16
    mix = _dot(attn_ref[...], woa_ref[...]) + _dot(pool_ref[...], wop_ref[...])
    x1 = _layer_norm(ALPHA * x_ref[...] + mix, g1_ref[...], b1_ref[...])
    xb = x1.astype(bf)
    gate, g_sel = _router_gates(x1, xb, wr_ref, br_ref)
    tm = gate.shape[0]
    flag = lambda cond: jnp.where(cond, 1.0, 0.0)

    lane = lax.broadcasted_iota(jnp.int32, (tm, LANES), 1)
    onehot = flag(lane == g_sel)
    counts = jnp.sum(onehot, axis=0, keepdims=True)
    earlier = flag(lax.broadcasted_iota(jnp.int32, (tm, tm), 1)
                   < lax.broadcasted_iota(jnp.int32, (tm, tm), 0)).astype(bf)
    before = _dot(earlier, onehot.astype(bf))
    rank = jnp.sum(onehot * before, axis=1, keepdims=True)
    start = jnp.sum(jnp.where(lane < g_sel, counts, 0.0), axis=1, keepdims=True)
    pos = start + rank

    pos_hi = jnp.floor(pos * (1.0 / POS_RADIX))
    digits = jnp.where(lane == 0, pos_hi, jnp.where(lane == 1, pos - POS_RADIX * pos_hi, 0.0)).astype(bf)
    pick = flag(lax.broadcasted_iota(jnp.int32, (PACKED_SUBLANES, LANES), 0)
                == lax.broadcasted_iota(jnp.int32, (PACKED_SUBLANES, LANES), 1)).astype(bf)
    digits_row = _dot_nt(pick, digits)
    pos_row = POS_RADIX * digits_row[0:1] + digits_row[1:2]

    lane1 = lax.broadcasted_iota(jnp.int32, (1, LANES), 1)
    group_count = [jnp.sum(jnp.where(lane1 == g, counts, 0.0)) for g in range(N_EXPERT_GROUPS)]
    group_start = [jnp.sum(jnp.where(lane1 < g, counts, 0.0)) for g in range(N_EXPERT_GROUPS)]
    gate_parts = _split_bf16(gate, 3)
    lane_rt = lax.broadcasted_iota(jnp.int32, (ROW_TILE, LANES), 1)

    ys_ref[...] = jnp.zeros_like(ys_ref)
    for r in range(tm // ROW_TILE):
        row0 = r * ROW_TILE
        tile_rows = slice(row0, row0 + ROW_TILE)
        sorted_row = (row0 + lax.broadcasted_iota(jnp.int32, (ROW_TILE, tm), 0)).astype(f32)
        perm = flag(sorted_row == pos_row).astype(bf)
        xs = _dot(perm, xb).astype(bf)
        gate_s = functools.reduce(lambda a, b: a + b, [_dot(perm, p) for p in gate_parts])
        for g in range(N_EXPERT_GROUPS):
            occurs = jnp.logical_and(
                group_count[g] > 0.0,
                jnp.logical_and(group_start[g] < row0 + ROW_TILE, group_start[g] + group_count[g] > row0))

            @pl.when(occurs)
            def _(g=g, xs=xs, gate_s=gate_s, tile_rows=tile_rows):
                for e in range(EXPERTS_PER_GROUP):
                    ex = g * EXPERTS_PER_GROUP + e
                    a = _dot(xs, wg_ref[ex])
                    b = _dot(xs, wu_ref[ex])
                    gcol = jnp.sum(jnp.where(lane_rt == ex, gate_s, 0.0), axis=1, keepdims=True)
                    hid = (a * jax.nn.sigmoid(a)) * b * gcol
                    hid_ref[:, e * D_EXPERT:(e + 1) * D_EXPERT] = hid.astype(bf)
                rows = slice(g * EXPERTS_PER_GROUP * D_EXPERT, (g + 1) * EXPERTS_PER_GROUP * D_EXPERT)
                ys_ref[tile_rows, :] += _dot(hid_ref[...], wd_ref[rows, :])

    unperm = flag(lax.broadcasted_iota(jnp.int32, (tm, tm), 1).astype(f32) == pos).astype(bf)
    y = functools.reduce(lambda a, b: a + b, [_dot(unperm, piece) for piece in _split_bf16(ys_ref[...], 2)])
    out_ref[...] = _layer_norm(ALPHA * x1 + y, g2_ref[...], b2_ref[...])


def _output_and_experts(attn, pool, x2d, w_out, ln1_g, ln1_b, w_gr, b_gr, w_er, b_er,
                        w_gate, w_up, w_down, ln2_g, ln2_b):
    T, D = x2d.shape
    tm = TM_TAIL
    bf = jnp.bfloat16
    woa = w_out[:Q_COLS].astype(bf)
    wop = w_out[Q_COLS:].astype(bf)
    w_r = jnp.pad(jnp.concatenate([w_er, w_gr], axis=1), ((0, 0), (0, LANES - N_EXPERTS - N_EXPERT_GROUPS)))
    w_r_hi = w_r.astype(bf)
    w_r_lo = (w_r - w_r_hi.astype(jnp.float32)).astype(bf)
    wr = jnp.concatenate([w_r_hi, w_r_lo], axis=1)
    br = jnp.pad(jnp.concatenate([b_er, b_gr]), (0, LANES - N_EXPERTS - N_EXPERT_GROUPS)).reshape(1, LANES)
    wg = w_gate.reshape(N_EXPERTS, D, D_EXPERT).astype(bf)
    wu = w_up.reshape(N_EXPERTS, D, D_EXPERT).astype(bf)
    wd = w_down.reshape(N_EXPERTS * D_EXPERT, D).astype(bf)
    resident = lambda a: pl.BlockSpec(a.shape, lambda i: (0,) * a.ndim, pipeline_mode=pl.Buffered(1))
    consts = (woa, wop, ln1_g.reshape(1, D), ln1_b.reshape(1, D), wr, br,
              wg, wu, wd, ln2_g.reshape(1, D), ln2_b.reshape(1, D))
    return pl.pallas_call(
        _tail_kernel,
        grid=(T // tm,),
        in_specs=[pl.BlockSpec((tm, Q_COLS), lambda i: (i, 0)),
                  pl.BlockSpec((tm, D_POOL), lambda i: (i, 0)),
                  pl.BlockSpec((tm, D), lambda i: (i, 0))] + [resident(c) for c in consts],
        out_specs=pl.BlockSpec((tm, D), lambda i: (i, 0)),
        out_shape=jax.ShapeDtypeStruct((T, D), jnp.float32),
        scratch_shapes=[pltpu.VMEM((ROW_TILE, EXPERTS_PER_GROUP * D_EXPERT), bf),
                        pltpu.VMEM((tm, D), jnp.float32)],
        compiler_params=pltpu.CompilerParams(
            dimension_semantics=("arbitrary",), vmem_limit_bytes=VMEM_LIMIT_BYTES),
        name="dsa_output_and_experts",
    )(attn, pool, x2d, *consts)


def kernel(x, w_in, w_pool, pool_scale, w_out, ln1_g, ln1_b, w_group_router, b_group_router,
           w_expert_router, b_expert_router, w_gate, w_up, w_down, ln2_g, ln2_b):
    B, S, D = x.shape
    assert D == D_MODEL and S % TM_PROJ == 0 and S % TQ == 0 and TQ == TK
    assert w_in.shape[0] == DEPTH == 1
    l = 0
    qT, vT, qiT, wiT, k, ki, pool = _input_projection(x, w_in[l], w_pool[l], pool_scale[l])
    attn = _dsa_attention(qT, qiT, wiT, k, ki, vT)
    x2d = x.reshape(B * S, D)
    out = _output_and_experts(
        attn.reshape(B * S, Q_COLS), pool.reshape(B * S, D_POOL), x2d, w_out[l], ln1_g[l], ln1_b[l],
        w_group_router[l], b_group_router[l], w_expert_router[l], b_expert_router[l],
        w_gate[l], w_up[l], w_down[l], ln2_g[l], ln2_b[l])
    return out.reshape(B, S, D)
```

```python
import functools
import math

import jax
import jax.numpy as jnp
from jax import lax
from jax.experimental import pallas as pl
from jax.experimental.pallas import tpu as pltpu

D_MODEL = 1024
N_HEADS = 8
HEAD_DIM = 64
N_KV_HEADS = 2
Q_PER_KV = N_HEADS // N_KV_HEADS
IDX_HEADS = 8
IDX_DIM = 64
TOPK_MAX = 256
POOL_WINDOWS = (2, 4, 8, 16)
N_POOL_GROUPS = len(POOL_WINDOWS)
POOL_GROUP_DIM = 128
D_POOL = N_POOL_GROUPS * POOL_GROUP_DIM
MAX_WIN = max(POOL_WINDOWS)
Q_COLS = N_HEADS * HEAD_DIM
KV_COLS = N_KV_HEADS * HEAD_DIM
QI_COLS = IDX_HEADS * IDX_DIM
N_EXPERT_GROUPS = 4
EXPERTS_PER_GROUP = 4
N_EXPERTS = N_EXPERT_GROUPS * EXPERTS_PER_GROUP
D_EXPERT = 256
DEPTH = 1
ALPHA = float((2 * DEPTH) ** 0.25)
LN_EPS = 1e-5

LANES = 128
SUBLANES = 8
MXU_DIM = 256
VMEM_LIMIT_BYTES = 56 * 1024 * 1024

TM_PROJ = 1024
TQ = MXU_DIM
TK = MXU_DIM
TK_PIECE = LANES
STAGE_LEAD = 2
TM_TAIL = 512
ROW_TILE = LANES
POS_RADIX = 16

F32_MAX = float(jnp.finfo(jnp.float32).max)
KEEP_ALL = 1.0e9
PACKED_SUBLANES = 2 * SUBLANES
V_ROWS = HEAD_DIM + PACKED_SUBLANES
BF16_ULP_BOUND = 2.0 ** -7
F32_TINY = 2.0 ** -120
COARSE_ITERS = 8
FINE_ITERS_UNCHECKED = 7

_NT_DIMS = (((1,), (1,)), ((), ()))


def _dot(a, b):
    return jnp.dot(a, b, preferred_element_type=jnp.float32)


def _dot_nt(a, b):
    return lax.dot_general(a, b, _NT_DIMS, preferred_element_type=jnp.float32)


def _proj_kernel(x_ref, wqT_ref, wvT_ref, wqiT_ref, wwiT_ref, wnat_ref, wpool_ref, pscale_ref,
                 qT_ref, vT_ref, qiT_ref, wiT_ref, k_ref, ki_ref, pool_ref,
                 hist_ref, ext_ref, *, idx_scale):
    s_blk = pl.program_id(1)
    tm = x_ref.shape[1]
    xb = x_ref[0].astype(jnp.bfloat16)

    qT_ref[0] = _dot_nt(wqT_ref[...], xb).astype(jnp.bfloat16)
    qiT_ref[0] = _dot_nt(wqiT_ref[...], xb).astype(jnp.bfloat16)
    wiT_ref[0] = _dot_nt(wwiT_ref[...], xb)[:IDX_HEADS] * idx_scale
    vT = _dot_nt(wvT_ref[...], xb).astype(jnp.bfloat16)
    for g in range(N_KV_HEADS):
        for j in range(tm // TK):
            vT_ref[0, g, j, :HEAD_DIM, :] = vT[g * HEAD_DIM:(g + 1) * HEAD_DIM, j * TK:(j + 1) * TK]
            vT_ref[0, g, j, HEAD_DIM:, :] = jnp.ones((V_ROWS - HEAD_DIM, TK), jnp.bfloat16)

    nat = _dot(xb, wnat_ref[...])
    o = D_POOL
    k_ref[0, 0] = nat[:, o:o + HEAD_DIM].astype(jnp.bfloat16)
    k_ref[0, 1] = nat[:, o + HEAD_DIM:o + KV_COLS].astype(jnp.bfloat16)
    ki_ref[0] = nat[:, o + KV_COLS:o + KV_COLS + IDX_DIM].astype(jnp.bfloat16)

    u = nat[:, :D_POOL]

    @pl.when(s_blk == 0)
    def _():
        hist_ref[...] = jnp.zeros_like(hist_ref)

    ext_ref[0:MAX_WIN, :] = hist_ref[...]
    ext_ref[MAX_WIN:MAX_WIN + tm, :] = u
    hist_ref[...] = u[tm - MAX_WIN:tm, :]

    pos = s_blk * tm + lax.broadcasted_iota(jnp.int32, (tm, POOL_GROUP_DIM), 0)
    for g, win in enumerate(POOL_WINDOWS):
        c0 = g * POOL_GROUP_DIM
        acc = ext_ref[MAX_WIN:MAX_WIN + tm, c0:c0 + POOL_GROUP_DIM]
        for j in range(1, win):
            acc = acc + ext_ref[MAX_WIN - j:MAX_WIN - j + tm, c0:c0 + POOL_GROUP_DIM]
        count = jnp.minimum(pos + 1, win).astype(jnp.float32)
        d = acc / count - ext_ref[MAX_WIN:MAX_WIN + tm, c0:c0 + POOL_GROUP_DIM]
        y = _dot(d.astype(jnp.bfloat16), wpool_ref[g]) * pscale_ref[:, c0:c0 + POOL_GROUP_DIM]
        pool_ref[0, :, c0:c0 + POOL_GROUP_DIM] = y.astype(jnp.bfloat16)


def _input_projection(x, w_in, w_pool, pool_scale):
    B, S, D = x.shape
    tm = TM_PROJ
    bf = jnp.bfloat16
    idx_scale = (IDX_DIM ** -0.5) * (IDX_HEADS ** -0.5)
    q_scale = (HEAD_DIM ** -0.5) * math.log2(math.e)

    o = 0
    w_q = w_in[:, o:o + Q_COLS]; o += Q_COLS
    w_k = w_in[:, o:o + KV_COLS]; o += KV_COLS
    w_v = w_in[:, o:o + KV_COLS]; o += KV_COLS
    w_pin = w_in[:, o:o + D_POOL]; o += D_POOL
    w_qi = w_in[:, o:o + QI_COLS]; o += QI_COLS
    w_ki = w_in[:, o:o + IDX_DIM]; o += IDX_DIM
    w_wi = w_in[:, o:o + IDX_HEADS]

    wqT = (w_q * q_scale).T.astype(bf)
    wvT = w_v.T.astype(bf)
    wqiT = w_qi.T.astype(bf)
    wwiT = jnp.pad(w_wi.T, ((0, 2 * SUBLANES - IDX_HEADS), (0, 0))).astype(bf)
    wnat = jnp.concatenate([w_pin, w_k, w_ki], axis=1).astype(bf)

    full = lambda a: pl.BlockSpec(a.shape, lambda b, s: (0,) * a.ndim)
    weights = (wqT, wvT, wqiT, wwiT, wnat, w_pool.astype(bf), pool_scale.reshape(1, D_POOL))
    out_shape = (
        jax.ShapeDtypeStruct((B, Q_COLS, S), bf),
        jax.ShapeDtypeStruct((B, N_KV_HEADS, S // TK, V_ROWS, TK), bf),
        jax.ShapeDtypeStruct((B, QI_COLS, S), bf),
        jax.ShapeDtypeStruct((B, IDX_HEADS, S), jnp.float32),
        jax.ShapeDtypeStruct((B, N_KV_HEADS, S, HEAD_DIM), bf),
        jax.ShapeDtypeStruct((B, S, IDX_DIM), bf),
        jax.ShapeDtypeStruct((B, S, D_POOL), bf),
    )
    out_specs = (
        pl.BlockSpec((1, Q_COLS, tm), lambda b, s: (b, 0, s)),
        pl.BlockSpec((1, N_KV_HEADS, tm // TK, V_ROWS, TK), lambda b, s: (b, 0, s, 0, 0)),
        pl.BlockSpec((1, QI_COLS, tm), lambda b, s: (b, 0, s)),
        pl.BlockSpec((1, IDX_HEADS, tm), lambda b, s: (b, 0, s)),
        pl.BlockSpec((1, N_KV_HEADS, tm, HEAD_DIM), lambda b, s: (b, 0, s, 0)),
        pl.BlockSpec((1, tm, IDX_DIM), lambda b, s: (b, s, 0)),
        pl.BlockSpec((1, tm, D_POOL), lambda b, s: (b, s, 0)),
    )
    return pl.pallas_call(
        functools.partial(_proj_kernel, idx_scale=idx_scale),
        grid=(B, S // tm),
        in_specs=[pl.BlockSpec((1, tm, D), lambda b, s: (b, s, 0))] + [full(w) for w in weights],
        out_specs=out_specs,
        out_shape=out_shape,
        scratch_shapes=[pltpu.VMEM((MAX_WIN, D_POOL), jnp.float32),
                        pltpu.VMEM((MAX_WIN + tm, D_POOL), jnp.float32)],
        compiler_params=pltpu.CompilerParams(
            dimension_semantics=("arbitrary", "arbitrary"), vmem_limit_bytes=VMEM_LIMIT_BYTES),
        name="dsa_input_projection",
    )(x, *weights)


def _dsa_kernel(qT_ref, qiT_ref, wiT_ref, k_ref, ki_ref, vT_ref, out_ref,
                scr_ref, s16_ref, s0_ref, sa_ref, sb_ref, m_ref, acc_ref):
    q_blk = pl.program_id(1)
    n_chunks = q_blk + 1
    q0 = q_blk * TQ
    topk = float(TOPK_MAX)
    neg_inf = -jnp.inf

    row = lambda v: jnp.full((1, TQ), v, jnp.float32)
    flag = lambda cond: jnp.where(cond, 1.0, 0.0)
    qpos = q0 + lax.broadcasted_iota(jnp.int32, (1, TQ), 1)
    kiota = lax.broadcasted_iota(jnp.int32, (TK, TQ), 0)

    def load_chunk(c):
        return scr_ref[c]

    def store_chunk(c, val):
        scr_ref[c] = val

    N_ACC = 4

    def fold(x):
        parts = x.reshape(N_ACC, x.shape[0] // (N_ACC * SUBLANES), SUBLANES, TQ)
        return tuple(jnp.sum(parts[i], axis=0) for i in range(N_ACC))

    def add_folded(accs, x):
        return tuple(a + f for a, f in zip(accs, fold(x)))

    def finish(accs):
        return jnp.sum(functools.reduce(lambda a, b: a + b, accs), axis=0, keepdims=True)

    zero_accs = tuple(jnp.zeros((SUBLANES, TQ), jnp.float32) for _ in range(N_ACC))

    def score_piece(c, r0, carry, diagonal):
        mn, mx, a_ge, a_gt = carry
        rows = slice(r0, r0 + TK_PIECE)
        kic = ki_ref[0, c, rows, :]
        sc = jnp.zeros((TK_PIECE, TQ), jnp.float32)
        for h in range(IDX_HEADS):
            z = _dot(kic, qiT_ref[0, h * IDX_DIM:(h + 1) * IDX_DIM, :])
            sc = sc + jnp.maximum(z, 0.0) * wiT_ref[0, h:h + 1, :]
        sc_hi = sc
        if diagonal:
            valid = (q0 + r0 + lax.broadcasted_iota(jnp.int32, (TK_PIECE, TQ), 0)) <= qpos
            sc_hi = jnp.where(valid, sc, jnp.inf)
            sc = jnp.where(valid, sc, neg_inf)
        scr_ref[c, rows, :] = sc
        s16_ref[c, rows, :] = sc.astype(jnp.bfloat16)
        mx = jnp.maximum(mx, jnp.max(sc, axis=0, keepdims=True))
        mn = jnp.minimum(mn, jnp.min(sc_hi, axis=0, keepdims=True))
        return mn, mx, add_folded(a_ge, flag(sc >= 0.0)), add_folded(a_gt, flag(sc > 0.0))

    def score_chunk(c, carry, diagonal=False):
        for r0 in range(0, TK, TK_PIECE):
            carry = score_piece(c, r0, carry, diagonal)
        return carry

    def score_two(i, carry):
        return score_chunk(2 * i + 1, score_chunk(2 * i, carry))
    carry = lax.fori_loop(0, q_blk // 2, score_two, (row(jnp.inf), row(neg_inf), zero_accs, zero_accs))
    carry = lax.cond(q_blk % 2 == 1, lambda cr: score_chunk(q_blk - 1, cr), lambda cr: cr, carry)
    smin, smax, a_ge, a_gt = score_chunk(q_blk, carry, diagonal=True)
    c_ge0 = finish(a_ge)
    c_gt0 = finish(a_gt)

    def count_ge(p):
        def body(c, accs):
            return add_folded(accs, flag(load_chunk(c) >= p))
        return finish(lax.fori_loop(0, n_chunks, body, zero_accs))

    n_valid = (qpos + 1).astype(jnp.float32)
    short = flag(n_valid <= topk)
    at_zero = (1.0 - short) * flag(c_ge0 >= topk) * flag(c_gt0 <= topk)
    above = (1.0 - short) * flag(c_gt0 > topk)
    below = (1.0 - short) * flag(c_ge0 < topk)

    thr = jnp.where(short > 0.0, -F32_MAX, 0.0)
    keep = jnp.where(at_zero * flag(c_ge0 > topk) > 0.0, topk - c_gt0, KEEP_ALL)
    done = jnp.maximum(short, at_zero)
    lo = jnp.where(above > 0.0, 0.0, smin)
    hi = jnp.where(below > 0.0, 0.0, jnp.inf)
    c_hi = jnp.where(below > 0.0, c_ge0, 0.0)

    def coarse_step(_, st):
        lo, hi = st
        p16 = (0.5 * lo + 0.5 * jnp.minimum(hi, smax)).astype(jnp.bfloat16)
        p = p16.astype(jnp.float32)
        inside = flag(p > lo) * flag(p < hi)
        def body(c, accs):
            hit = jnp.where(s16_ref[c] >= p16, one16, zero16)
            parts = hit.reshape(N_ACC, TK // (N_ACC * PACKED_SUBLANES), PACKED_SUBLANES, TQ)
            return tuple(functools.reduce(lambda x, y: x + y, [a] + [parts[i, r] for r in range(parts.shape[1])])
                         for i, a in enumerate(accs))
        accs = lax.fori_loop(0, n_chunks, body, zero_accs16)
        c = finish(tuple(a.astype(jnp.float32) for a in accs))
        lo = jnp.where(inside * flag(c >= topk) > 0.0, p, lo)
        hi = jnp.where(inside * flag(c < topk) > 0.0, p, hi)
        return lo, hi

    one16 = jnp.ones((), jnp.bfloat16)
    zero16 = jnp.zeros((), jnp.bfloat16)
    zero_accs16 = tuple(jnp.zeros((PACKED_SUBLANES, TQ), jnp.bfloat16) for _ in range(N_ACC))
    lo, hi_coarse = lax.fori_loop(0, COARSE_ITERS, coarse_step, (lo, hi))
    lo = lo - (jnp.abs(lo) * BF16_ULP_BOUND + F32_TINY)

    def probe(p, active, lo, hi, c_hi, thr, done):
        c = count_ge(p)
        is_eq = active * flag(c == topk)
        is_gt = active * flag(c > topk)
        is_lt = active * flag(c < topk)
        thr = jnp.where(is_eq > 0.0, p, thr)
        done = jnp.maximum(done, is_eq)
        lo = jnp.where(is_gt > 0.0, p, lo)
        hi = jnp.where(is_lt > 0.0, p, hi)
        c_hi = jnp.where(is_lt > 0.0, c, c_hi)
        return lo, hi, c_hi, thr, done

    def bisect_step(lo, hi, c_hi, thr, done, parked):
        p = 0.5 * lo + 0.5 * jnp.minimum(hi, smax)
        parked = jnp.maximum(parked, 1.0 - flag(p > lo) * flag(p < hi))
        parked = jnp.maximum(parked, flag(c_hi == topk - 1.0))
        active = flag(done + parked <= 0.0)
        return probe(p, active, lo, hi, c_hi, thr, done) + (parked,)

    lo, hi, c_hi, thr, done = probe(jnp.minimum(hi_coarse, smax), 1.0 - done, lo, hi, c_hi, thr, done)
    lo, hi, c_hi, thr, done, _ = lax.fori_loop(
        0, FINE_ITERS_UNCHECKED, lambda _, v: bisect_step(*v), (lo, hi, c_hi, thr, done, row(0.0)))

    def walk_step(hi, c_hi, thr, keep, done):
        def below_max(c, acc):
            blk = load_chunk(c)
            part = jnp.where(blk < hi, blk, neg_inf)
            return jnp.maximum(acc, jnp.max(part.reshape(TK // SUBLANES, SUBLANES, TQ), axis=0))
        m8 = lax.fori_loop(0, n_chunks, below_max, jnp.full((SUBLANES, TQ), neg_inf, jnp.float32))
        m = jnp.max(m8, axis=0, keepdims=True)
        c = count_ge(m)
        active = 1.0 - done
        reach = active * flag(c >= topk)
        thr = jnp.where(reach > 0.0, m, thr)
        keep = jnp.where(reach * flag(c > topk) > 0.0, topk - c_hi, keep)
        done = jnp.maximum(done, reach)
        step = active * (1.0 - reach)
        hi = jnp.where(step > 0.0, m, hi)
        c_hi = jnp.where(step > 0.0, c, c_hi)
        return hi, c_hi, thr, keep, done

    hi, c_hi, thr, keep, done = walk_step(hi, c_hi, thr, keep, done)

    def leftover_cond(st):
        return st[0] > 0.0

    def leftover_body(st):
        _, lo, hi, c_hi, thr, keep, done = st
        lo, hi, c_hi, thr, done, _ = bisect_step(*bisect_step(lo, hi, c_hi, thr, done, row(0.0)))
        hi, c_hi, thr, keep, done = walk_step(hi, c_hi, thr, keep, done)
        return (jnp.sum(1.0 - done), lo, hi, c_hi, thr, keep, done)

    st = (jnp.sum(1.0 - done), lo, hi, c_hi, thr, keep, done)
    _, _, _, _, thr, keep, _ = lax.while_loop(leftover_cond, leftover_body, st)

    has_ties = jnp.sum(flag(keep < KEEP_ALL)) > 0.0

    @pl.when(jnp.logical_not(has_ties))
    def _():
        def body(c, carry):
            store_chunk(c, jnp.where(load_chunk(c) >= thr, 0.0, neg_inf))
            return carry
        lax.fori_loop(0, n_chunks, body, 0)

    @pl.when(has_ties)
    def _():
        tri = flag(lax.broadcasted_iota(jnp.int32, (TK, TK), 1)
                   <= lax.broadcasted_iota(jnp.int32, (TK, TK), 0)).astype(jnp.bfloat16)
        def body(c, seen):
            blk = load_chunk(c)
            eq = flag(blk == thr)
            rank = seen + _dot(tri, eq.astype(jnp.bfloat16))
            sel = jnp.maximum(flag(blk > thr), eq * flag(rank <= keep))
            store_chunk(c, jnp.where(sel > 0.0, 0.0, neg_inf))
            return seen + finish(fold(eq))
        lax.fori_loop(0, n_chunks, body, row(0.0))

    m_ref[...] = jnp.full_like(m_ref, neg_inf)
    acc_ref[...] = jnp.zeros_like(acc_ref)

    last = n_chunks - 1

    def stage_head(h, c, bias, dst_ref):
        dst_ref[h] = bias + _dot(k_ref[0, h // Q_PER_KV, c], qT_ref[0, h * HEAD_DIM:(h + 1) * HEAD_DIM, :])

    def softmax_head(h, c, src_ref):
        m_old = m_ref[h]
        m_new = jnp.maximum(m_old, jnp.max(src_ref[h], axis=0, keepdims=True))
        m_safe = jnp.where(m_new == neg_inf, 0.0, m_new)
        corr = jnp.exp2(m_old - m_safe)
        p = jnp.exp2(src_ref[h] - m_safe)
        acc_ref[h] = acc_ref[h] * corr + _dot(vT_ref[0, h // Q_PER_KV, c], p.astype(jnp.bfloat16))
        m_ref[h] = m_new

    def step(c, src_ref, c_next, dst_ref):
        bias_next = scr_ref[c_next]
        for h in range(STAGE_LEAD):
            stage_head(h, c_next, bias_next, dst_ref)
        for h in range(N_HEADS):
            softmax_head(h, c, src_ref)
            if h + STAGE_LEAD < N_HEADS:
                stage_head(h + STAGE_LEAD, c_next, bias_next, dst_ref)

    odd = n_chunks % 2

    @pl.when(odd == 1)
    def _():
        bias0 = scr_ref[0]
        for h in range(N_HEADS):
            stage_head(h, 0, bias0, s0_ref)
        step(0, s0_ref, jnp.minimum(1, last), sa_ref)

    @pl.when(odd == 0)
    def _():
        bias0 = scr_ref[0]
        for h in range(N_HEADS):
            stage_head(h, 0, bias0, sa_ref)

    def attend_pair(i, carry):
        c0 = odd + 2 * i
        step(c0, sa_ref, c0 + 1, sb_ref)
        step(c0 + 1, sb_ref, jnp.minimum(c0 + 2, last), sa_ref)
        return carry

    lax.fori_loop(0, n_chunks // 2, attend_pair, 0)
    outT = jnp.concatenate(
        [acc_ref[h, :HEAD_DIM, :] / acc_ref[h, HEAD_DIM:HEAD_DIM + 1, :] for h in range(N_HEADS)], axis=0)
    out_ref[0] = outT.T.astype(jnp.bfloat16)


def _dsa_attention(qT, qiT, wiT, k, ki, vT):
    B, _, S = qT.shape
    k = k.reshape(B, N_KV_HEADS, S // TK, TK, HEAD_DIM)
    ki = ki.reshape(B, S // TK, TK, IDX_DIM)
    per_batch = lambda a: pl.BlockSpec((1,) + a.shape[1:], lambda b, i: (b,) + (0,) * (a.ndim - 1))
    return pl.pallas_call(
        _dsa_kernel,
        grid=(B, S // TQ),
        in_specs=[
            pl.BlockSpec((1, Q_COLS, TQ), lambda b, i: (b, 0, i)),
            pl.BlockSpec((1, QI_COLS, TQ), lambda b, i: (b, 0, i)),
            pl.BlockSpec((1, IDX_HEADS, TQ), lambda b, i: (b, 0, i)),
            per_batch(k), per_batch(ki), per_batch(vT),
        ],
        out_specs=pl.BlockSpec((1, TQ, Q_COLS), lambda b, i: (b, i, 0)),
        out_shape=jax.ShapeDtypeStruct((B, S, Q_COLS), jnp.bfloat16),
        scratch_shapes=[
            pltpu.VMEM((S // TK, TK, TQ), jnp.float32),
            pltpu.VMEM((S // TK, TK, TQ), jnp.bfloat16),
            pltpu.VMEM((N_HEADS, TK, TQ), jnp.float32),
            pltpu.VMEM((N_HEADS, TK, TQ), jnp.float32),
            pltpu.VMEM((N_HEADS, TK, TQ), jnp.float32),
            pltpu.VMEM((N_HEADS, 1, TQ), jnp.float32),
            pltpu.VMEM((N_HEADS, V_ROWS, TQ), jnp.float32),
        ],
        compiler_params=pltpu.CompilerParams(
            dimension_semantics=("arbitrary", "arbitrary"), vmem_limit_bytes=VMEM_LIMIT_BYTES),
        name="dsa_attention",
    )(qT, qiT, wiT, k, ki, vT)


def _layer_norm(z, g, b):
    mu = jnp.mean(z, axis=-1, keepdims=True)
    zc = z - mu
    var = jnp.mean(zc * zc, axis=-1, keepdims=True)
    return zc * lax.rsqrt(var + LN_EPS) * g + b


def _router_gates(x1, hi, wr_ref, br_ref):
    lo = (x1 - hi.astype(jnp.float32)).astype(jnp.bfloat16)

    r = _dot(hi, wr_ref[...])
    logits = r[:, :LANES] + r[:, LANES:] + _dot(lo, wr_ref[:, :LANES]) + br_ref[...]

    tm = logits.shape[0]
    lane = lax.broadcasted_iota(jnp.int32, (tm, LANES), 1)
    neg_inf = -jnp.inf
    is_grp = jnp.where(lane >= N_EXPERTS, jnp.where(lane < N_EXPERTS + N_EXPERT_GROUPS, 1.0, 0.0), 0.0)
    gl = jnp.where(is_grp > 0.0, logits, neg_inf)
    g_max = jnp.max(gl, axis=1, keepdims=True)
    g_sel = jnp.min(jnp.where(gl == g_max, lane, LANES), axis=1, keepdims=True) - N_EXPERTS
    g_weight = 1.0 / jnp.sum(jnp.exp(gl - g_max), axis=1, keepdims=True)

    in_grp = jnp.right_shift(lane, EXPERTS_PER_GROUP.bit_length() - 1) == g_sel
    el = jnp.where(in_grp, logits, neg_inf)
    v1 = jnp.max(el, axis=1, keepdims=True)
    i1 = jnp.min(jnp.where(el == v1, lane, LANES), axis=1, keepdims=True)
    el2 = jnp.where(lane == i1, neg_inf, el)
    v2 = jnp.max(el2, axis=1, keepdims=True)
    i2 = jnp.min(jnp.where(el2 == v2, lane, LANES), axis=1, keepdims=True)
    e2 = jnp.exp(v2 - v1)
    w1 = g_weight / (1.0 + e2)
    w2 = g_weight * e2 / (1.0 + e2)
    return jnp.where(lane == i1, w1, jnp.where(lane == i2, w2, 0.0)), g_sel


def _split_bf16(v, parts):
    out = []
    for _ in range(parts):
        piece = v.astype(jnp.bfloat16)
        out.append(piece)
        v = v - piece.astype(jnp.float32)
    return out


def _tail_kernel(attn_ref, pool_ref, x_ref, woa_ref, wop_ref, g1_ref, b1_ref, wr_ref, br_ref,
                 wg_ref, wu_ref, wd_ref, g2_ref, b2_ref, out_ref, hid_ref, ys_ref):
    f32, bf = jnp.float32, jnp.bfloat16
    mix = _dot(attn_ref[...], woa_ref[...]) + _dot(pool_ref[...], wop_ref[...])
    x1 = _layer_norm(ALPHA * x_ref[...] + mix, g1_ref[...], b1_ref[...])
    xb = x1.astype(bf)
    gate, g_sel = _router_gates(x1, xb, wr_ref, br_ref)
    tm = gate.shape[0]
    flag = lambda cond: jnp.where(cond, 1.0, 0.0)

    lane = lax.broadcasted_iota(jnp.int32, (tm, LANES), 1)
    onehot = flag(lane == g_sel)
    counts = jnp.sum(onehot, axis=0, keepdims=True)
    earlier = flag(lax.broadcasted_iota(jnp.int32, (tm, tm), 1)
                   < lax.broadcasted_iota(jnp.int32, (tm, tm), 0)).astype(bf)
    before = _dot(earlier, onehot.astype(bf))
    rank = jnp.sum(onehot * before, axis=1, keepdims=True)
    start = jnp.sum(jnp.where(lane < g_sel, counts, 0.0), axis=1, keepdims=True)
    pos = start + rank

    pos_hi = jnp.floor(pos * (1.0 / POS_RADIX))
    digits = jnp.where(lane == 0, pos_hi, jnp.where(lane == 1, pos - POS_RADIX * pos_hi, 0.0)).astype(bf)
    pick = flag(lax.broadcasted_iota(jnp.int32, (PACKED_SUBLANES, LANES), 0)
                == lax.broadcasted_iota(jnp.int32, (PACKED_SUBLANES, LANES), 1)).astype(bf)
    digits_row = _dot_nt(pick, digits)
    pos_row = POS_RADIX * digits_row[0:1] + digits_row[1:2]

    lane1 = lax.broadcasted_iota(jnp.int32, (1, LANES), 1)
    group_count = [jnp.sum(jnp.where(lane1 == g, counts, 0.0)) for g in range(N_EXPERT_GROUPS)]
    group_start = [jnp.sum(jnp.where(lane1 < g, counts, 0.0)) for g in range(N_EXPERT_GROUPS)]
    gate_parts = _split_bf16(gate, 3)
    lane_rt = lax.broadcasted_iota(jnp.int32, (ROW_TILE, LANES), 1)

    ys_ref[...] = jnp.zeros_like(ys_ref)
    for r in range(tm // ROW_TILE):
        row0 = r * ROW_TILE
        tile_rows = slice(row0, row0 + ROW_TILE)
        sorted_row = (row0 + lax.broadcasted_iota(jnp.int32, (ROW_TILE, tm), 0)).astype(f32)
        perm = flag(sorted_row == pos_row).astype(bf)
        xs = _dot(perm, xb).astype(bf)
        gate_s = functools.reduce(lambda a, b: a + b, [_dot(perm, p) for p in gate_parts])
        for g in range(N_EXPERT_GROUPS):
            occurs = jnp.logical_and(
                group_count[g] > 0.0,
                jnp.logical_and(group_start[g] < row0 + ROW_TILE, group_start[g] + group_count[g] > row0))

            @pl.when(occurs)
            def _(g=g, xs=xs, gate_s=gate_s, tile_rows=tile_rows):
                for e in range(EXPERTS_PER_GROUP):
                    ex = g * EXPERTS_PER_GROUP + e
                    a = _dot(xs, wg_ref[ex])
                    b = _dot(xs, wu_ref[ex])
                    gcol = jnp.sum(jnp.where(lane_rt == ex, gate_s, 0.0), axis=1, keepdims=True)
                    hid = (a * jax.nn.sigmoid(a)) * b * gcol
                    hid_ref[:, e * D_EXPERT:(e + 1) * D_EXPERT] = hid.astype(bf)
                rows = slice(g * EXPERTS_PER_GROUP * D_EXPERT, (g + 1) * EXPERTS_PER_GROUP * D_EXPERT)
                ys_ref[tile_rows, :] += _dot(hid_ref[...], wd_ref[rows, :])

    unperm = flag(lax.broadcasted_iota(jnp.int32, (tm, tm), 1).astype(f32) == pos).astype(bf)
    y = functools.reduce(lambda a, b: a + b, [_dot(unperm, piece) for piece in _split_bf16(ys_ref[...], 2)])
    out_ref[...] = _layer_norm(ALPHA * x1 + y, g2_ref[...], b2_ref[...])


def _output_and_experts(attn, pool, x2d, w_out, ln1_g, ln1_b, w_gr, b_gr, w_er, b_er,
                        w_gate, w_up, w_down, ln2_g, ln2_b):
    T, D = x2d.shape
    tm = TM_TAIL
    bf = jnp.bfloat16
    woa = w_out[:Q_COLS].astype(bf)
    wop = w_out[Q_COLS:].astype(bf)
    w_r = jnp.pad(jnp.concatenate([w_er, w_gr], axis=1), ((0, 0), (0, LANES - N_EXPERTS - N_EXPERT_GROUPS)))
    w_r_hi = w_r.astype(bf)
    w_r_lo = (w_r - w_r_hi.astype(jnp.float32)).astype(bf)
    wr = jnp.concatenate([w_r_hi, w_r_lo], axis=1)
    br = jnp.pad(jnp.concatenate([b_er, b_gr]), (0, LANES - N_EXPERTS - N_EXPERT_GROUPS)).reshape(1, LANES)
    wg = w_gate.reshape(N_EXPERTS, D, D_EXPERT).astype(bf)
    wu = w_up.reshape(N_EXPERTS, D, D_EXPERT).astype(bf)
    wd = w_down.reshape(N_EXPERTS * D_EXPERT, D).astype(bf)
    resident = lambda a: pl.BlockSpec(a.shape, lambda i: (0,) * a.ndim, pipeline_mode=pl.Buffered(1))
    consts = (woa, wop, ln1_g.reshape(1, D), ln1_b.reshape(1, D), wr, br,
              wg, wu, wd, ln2_g.reshape(1, D), ln2_b.reshape(1, D))
    return pl.pallas_call(
        _tail_kernel,
        grid=(T // tm,),
        in_specs=[pl.BlockSpec((tm, Q_COLS), lambda i: (i, 0)),
                  pl.BlockSpec((tm, D_POOL), lambda i: (i, 0)),
                  pl.BlockSpec((tm, D), lambda i: (i, 0))] + [resident(c) for c in consts],
        out_specs=pl.BlockSpec((tm, D), lambda i: (i, 0)),
        out_shape=jax.ShapeDtypeStruct((T, D), jnp.float32),
        scratch_shapes=[pltpu.VMEM((ROW_TILE, EXPERTS_PER_GROUP * D_EXPERT), bf),
                        pltpu.VMEM((tm, D), jnp.float32)],
        compiler_params=pltpu.CompilerParams(
            dimension_semantics=("arbitrary",), vmem_limit_bytes=VMEM_LIMIT_BYTES),
        name="dsa_output_and_experts",
    )(attn, pool, x2d, *consts)


def kernel(x, w_in, w_pool, pool_scale, w_out, ln1_g, ln1_b, w_group_router, b_group_router,
           w_expert_router, b_expert_router, w_gate, w_up, w_down, ln2_g, ln2_b):
    B, S, D = x.shape
    assert D == D_MODEL and S % TM_PROJ == 0 and S % TQ == 0 and TQ == TK
    assert w_in.shape[0] == DEPTH == 1
    l = 0
    qT, vT, qiT, wiT, k, ki, pool = _input_projection(x, w_in[l], w_pool[l], pool_scale[l])
    attn = _dsa_attention(qT, qiT, wiT, k, ki, vT)
    x2d = x.reshape(B * S, D)
    out = _output_and_experts(
        attn.reshape(B * S, Q_COLS), pool.reshape(B * S, D_POOL), x2d, w_out[l], ln1_g[l], ln1_b[l],
        w_group_router[l], b_group_router[l], w_expert_router[l], b_expert_router[l],
        w_gate[l], w_up[l], w_down[l], ln2_g[l], ln2_b[l])
    return out.reshape(B, S, D)
```

```python
import functools
import math

import jax
import jax.numpy as jnp
from jax import lax
from jax.experimental import pallas as pl
from jax.experimental.pallas import tpu as pltpu

D_MODEL = 1024
N_HEADS = 8
HEAD_DIM = 64
N_KV_HEADS = 2
Q_PER_KV = N_HEADS // N_KV_HEADS
IDX_HEADS = 8
IDX_DIM = 64
TOPK_MAX = 256
POOL_WINDOWS = (2, 4, 8, 16)
N_POOL_GROUPS = len(POOL_WINDOWS)
POOL_GROUP_DIM = 128
D_POOL = N_POOL_GROUPS * POOL_GROUP_DIM
MAX_WIN = max(POOL_WINDOWS)
Q_COLS = N_HEADS * HEAD_DIM
KV_COLS = N_KV_HEADS * HEAD_DIM
QI_COLS = IDX_HEADS * IDX_DIM
N_EXPERT_GROUPS = 4
EXPERTS_PER_GROUP = 4
N_EXPERTS = N_EXPERT_GROUPS * EXPERTS_PER_GROUP
D_EXPERT = 256
DEPTH = 1
ALPHA = float((2 * DEPTH) ** 0.25)
LN_EPS = 1e-5

LANES = 128
SUBLANES = 8
MXU_DIM = 256
VMEM_LIMIT_BYTES = 56 * 1024 * 1024

TM_PROJ = 1024
TQ = MXU_DIM
TK = MXU_DIM
TK_PIECE = LANES
STAGE_LEAD = 2
TM_TAIL = 512
ROW_TILE = LANES
POS_RADIX = 16

F32_MAX = float(jnp.finfo(jnp.float32).max)
KEEP_ALL = 1.0e9
PACKED_SUBLANES = 2 * SUBLANES
V_ROWS = HEAD_DIM + PACKED_SUBLANES
BF16_ULP_BOUND = 2.0 ** -7
F32_TINY = 2.0 ** -120
COARSE_ITERS = 8
FINE_ITERS_UNCHECKED = 7

_NT_DIMS = (((1,), (1,)), ((), ()))


def _dot(a, b):
    return jnp.dot(a, b, preferred_element_type=jnp.float32)


def _dot_nt(a, b):
    return lax.dot_general(a, b, _NT_DIMS, preferred_element_type=jnp.float32)


def _proj_kernel(x_ref, wqT_ref, wvT_ref, wqiT_ref, wwiT_ref, wnat_ref, wpool_ref, pscale_ref,
                 qT_ref, vT_ref, qiT_ref, wiT_ref, k_ref, ki_ref, pool_ref,
                 hist_ref, ext_ref, lvl_ref, *, idx_scale):
    s_blk = pl.program_id(1)
    tm = x_ref.shape[1]
    xb = x_ref[0].astype(jnp.bfloat16)

    qT_ref[0] = _dot_nt(wqT_ref[...], xb).astype(jnp.bfloat16)
    qiT_ref[0] = _dot_nt(wqiT_ref[...], xb).astype(jnp.bfloat16)
    wiT_ref[0] = _dot_nt(wwiT_ref[...], xb)[:IDX_HEADS] * idx_scale
    vT = _dot_nt(wvT_ref[...], xb).astype(jnp.bfloat16)
    for g in range(N_KV_HEADS):
        for j in range(tm // TK):
            vT_ref[0, g, j, :HEAD_DIM, :] = vT[g * HEAD_DIM:(g + 1) * HEAD_DIM, j * TK:(j + 1) * TK]
            vT_ref[0, g, j, HEAD_DIM:, :] = jnp.ones((V_ROWS - HEAD_DIM, TK), jnp.bfloat16)

    nat = _dot(xb, wnat_ref[...])
    o = D_POOL
    k_ref[0, 0] = nat[:, o:o + HEAD_DIM].astype(jnp.bfloat16)
    k_ref[0, 1] = nat[:, o + HEAD_DIM:o + KV_COLS].astype(jnp.bfloat16)
    ki_ref[0] = nat[:, o + KV_COLS:o + KV_COLS + IDX_DIM].astype(jnp.bfloat16)

    u = nat[:, :D_POOL]

    @pl.when(s_blk == 0)
    def _():
        hist_ref[...] = jnp.zeros_like(hist_ref)

    PAD = SUBLANES
    base, end = PAD + MAX_WIN, PAD + MAX_WIN + tm
    ext_ref[0:PAD, :] = jnp.zeros((PAD, D_POOL), jnp.float32)
    lvl_ref[0:PAD, :] = jnp.zeros((PAD, POOL_GROUP_DIM), jnp.float32)
    ext_ref[PAD:base, :] = hist_ref[...]
    ext_ref[base:end, :] = u
    hist_ref[...] = u[tm - MAX_WIN:tm, :]

    pos = s_blk * tm + lax.broadcasted_iota(jnp.int32, (tm, POOL_GROUP_DIM), 0)
    for g, win in enumerate(POOL_WINDOWS):
        c0 = g * POOL_GROUP_DIM
        cols = slice(c0, c0 + POOL_GROUP_DIM)
        level = ext_ref[PAD:end, cols] + ext_ref[PAD - 1:end - 1, cols]
        span = 2
        while span < win:
            lvl_ref[PAD:end, :] = level
            level = level + lvl_ref[PAD - span:end - span, :]
            span *= 2
        acc = level[MAX_WIN:MAX_WIN + tm]
        count = jnp.minimum(pos + 1, win).astype(jnp.float32)
        d = acc / count - ext_ref[base:end, cols]
        y = _dot(d.astype(jnp.bfloat16), wpool_ref[g]) * pscale_ref[:, c0:c0 + POOL_GROUP_DIM]
        pool_ref[0, :, c0:c0 + POOL_GROUP_DIM] = y.astype(jnp.bfloat16)


def _input_projection(x, w_in, w_pool, pool_scale):
    B, S, D = x.shape
    tm = TM_PROJ
    bf = jnp.bfloat16
    idx_scale = (IDX_DIM ** -0.5) * (IDX_HEADS ** -0.5)
    q_scale = (HEAD_DIM ** -0.5) * math.log2(math.e)

    o = 0
    w_q = w_in[:, o:o + Q_COLS]; o += Q_COLS
    w_k = w_in[:, o:o + KV_COLS]; o += KV_COLS
    w_v = w_in[:, o:o + KV_COLS]; o += KV_COLS
    w_pin = w_in[:, o:o + D_POOL]; o += D_POOL
    w_qi = w_in[:, o:o + QI_COLS]; o += QI_COLS
    w_ki = w_in[:, o:o + IDX_DIM]; o += IDX_DIM
    w_wi = w_in[:, o:o + IDX_HEADS]

    wqT = (w_q * q_scale).T.astype(bf)
    wvT = w_v.T.astype(bf)
    wqiT = w_qi.T.astype(bf)
    wwiT = jnp.pad(w_wi.T, ((0, 2 * SUBLANES - IDX_HEADS), (0, 0))).astype(bf)
    wnat = jnp.concatenate([w_pin, w_k, w_ki], axis=1).astype(bf)

    full = lambda a: pl.BlockSpec(a.shape, lambda b, s: (0,) * a.ndim)
    weights = (wqT, wvT, wqiT, wwiT, wnat, w_pool.astype(bf), pool_scale.reshape(1, D_POOL))
    out_shape = (
        jax.ShapeDtypeStruct((B, Q_COLS, S), bf),
        jax.ShapeDtypeStruct((B, N_KV_HEADS, S // TK, V_ROWS, TK), bf),
        jax.ShapeDtypeStruct((B, QI_COLS, S), bf),
        jax.ShapeDtypeStruct((B, IDX_HEADS, S), jnp.float32),
        jax.ShapeDtypeStruct((B, N_KV_HEADS, S, HEAD_DIM), bf),
        jax.ShapeDtypeStruct((B, S, IDX_DIM), bf),
        jax.ShapeDtypeStruct((B, S, D_POOL), bf),
    )
    out_specs = (
        pl.BlockSpec((1, Q_COLS, tm), lambda b, s: (b, 0, s)),
        pl.BlockSpec((1, N_KV_HEADS, tm // TK, V_ROWS, TK), lambda b, s: (b, 0, s, 0, 0)),
        pl.BlockSpec((1, QI_COLS, tm), lambda b, s: (b, 0, s)),
        pl.BlockSpec((1, IDX_HEADS, tm), lambda b, s: (b, 0, s)),
        pl.BlockSpec((1, N_KV_HEADS, tm, HEAD_DIM), lambda b, s: (b, 0, s, 0)),
        pl.BlockSpec((1, tm, IDX_DIM), lambda b, s: (b, s, 0)),
        pl.BlockSpec((1, tm, D_POOL), lambda b, s: (b, s, 0)),
    )
    return pl.pallas_call(
        functools.partial(_proj_kernel, idx_scale=idx_scale),
        grid=(B, S // tm),
        in_specs=[pl.BlockSpec((1, tm, D), lambda b, s: (b, s, 0))] + [full(w) for w in weights],
        out_specs=out_specs,
        out_shape=out_shape,
        scratch_shapes=[pltpu.VMEM((MAX_WIN, D_POOL), jnp.float32),
                        pltpu.VMEM((SUBLANES + MAX_WIN + tm, D_POOL), jnp.float32),
                        pltpu.VMEM((SUBLANES + MAX_WIN + tm, POOL_GROUP_DIM), jnp.float32)],
        compiler_params=pltpu.CompilerParams(
            dimension_semantics=("arbitrary", "arbitrary"), vmem_limit_bytes=VMEM_LIMIT_BYTES),
        name="dsa_input_projection",
    )(x, *weights)


def _dsa_kernel(qT_ref, qiT_ref, wiT_ref, k_ref, ki_ref, vT_ref, out_ref,
                scr_ref, s16_ref, s0_ref, sa_ref, sb_ref, m_ref, acc_ref):
    q_blk = pl.program_id(1)
    n_chunks = q_blk + 1
    q0 = q_blk * TQ
    topk = float(TOPK_MAX)
    neg_inf = -jnp.inf

    row = lambda v: jnp.full((1, TQ), v, jnp.float32)
    flag = lambda cond: jnp.where(cond, 1.0, 0.0)
    qpos = q0 + lax.broadcasted_iota(jnp.int32, (1, TQ), 1)

    def load_chunk(c):
        return scr_ref[c]

    def store_chunk(c, val):
        scr_ref[c] = val

    N_ACC = 4

    def fold(x):
        parts = x.reshape(N_ACC, x.shape[0] // (N_ACC * SUBLANES), SUBLANES, TQ)
        return tuple(jnp.sum(parts[i], axis=0) for i in range(N_ACC))

    def add_folded(accs, x):
        return tuple(a + f for a, f in zip(accs, fold(x)))

    def finish(accs):
        return jnp.sum(functools.reduce(lambda a, b: a + b, accs), axis=0, keepdims=True)

    zero_accs = tuple(jnp.zeros((SUBLANES, TQ), jnp.float32) for _ in range(N_ACC))

    def score_piece(c, r0, carry, diagonal):
        mn, mx, a_ge, a_gt = carry
        rows = slice(r0, r0 + TK_PIECE)
        kic = ki_ref[0, c, rows, :]
        sc = jnp.zeros((TK_PIECE, TQ), jnp.float32)
        for h in range(IDX_HEADS):
            z = _dot(kic, qiT_ref[0, h * IDX_DIM:(h + 1) * IDX_DIM, :])
            sc = sc + jnp.maximum(z, 0.0) * wiT_ref[0, h:h + 1, :]
        sc_hi = sc
        if diagonal:
            valid = (q0 + r0 + lax.broadcasted_iota(jnp.int32, (TK_PIECE, TQ), 0)) <= qpos
            sc_hi = jnp.where(valid, sc, jnp.inf)
            sc = jnp.where(valid, sc, neg_inf)
        scr_ref[c, rows, :] = sc
        s16_ref[c, rows, :] = sc.astype(jnp.bfloat16)
        mx = jnp.maximum(mx, jnp.max(sc, axis=0, keepdims=True))
        mn = jnp.minimum(mn, jnp.min(sc_hi, axis=0, keepdims=True))
        return mn, mx, add_folded(a_ge, flag(sc >= 0.0)), add_folded(a_gt, flag(sc > 0.0))

    def score_chunk(c, carry, diagonal=False):
        for r0 in range(0, TK, TK_PIECE):
            carry = score_piece(c, r0, carry, diagonal)
        return carry

    def score_two(i, carry):
        return score_chunk(2 * i + 1, score_chunk(2 * i, carry))
    carry = lax.fori_loop(0, q_blk // 2, score_two, (row(jnp.inf), row(neg_inf), zero_accs, zero_accs))
    carry = lax.cond(q_blk % 2 == 1, lambda cr: score_chunk(q_blk - 1, cr), lambda cr: cr, carry)
    smin, smax, a_ge, a_gt = score_chunk(q_blk, carry, diagonal=True)
    c_ge0 = finish(a_ge)
    c_gt0 = finish(a_gt)

    def count_ge(p):
        def body(c, accs):
            return add_folded(accs, flag(load_chunk(c) >= p))
        return finish(lax.fori_loop(0, n_chunks, body, zero_accs))

    n_valid = (qpos + 1).astype(jnp.float32)
    short = flag(n_valid <= topk)
    at_zero = (1.0 - short) * flag(c_ge0 >= topk) * flag(c_gt0 <= topk)
    above = (1.0 - short) * flag(c_gt0 > topk)
    below = (1.0 - short) * flag(c_ge0 < topk)

    thr = jnp.where(short > 0.0, -F32_MAX, 0.0)
    keep = jnp.where(at_zero * flag(c_ge0 > topk) > 0.0, topk - c_gt0, KEEP_ALL)
    done = jnp.maximum(short, at_zero)
    lo = jnp.where(above > 0.0, 0.0, smin)
    hi = jnp.where(below > 0.0, 0.0, jnp.inf)
    c_hi = jnp.where(below > 0.0, c_ge0, 0.0)

    def coarse_step(_, st):
        lo, hi = st
        p16 = (0.5 * lo + 0.5 * jnp.minimum(hi, smax)).astype(jnp.bfloat16)
        p = p16.astype(jnp.float32)
        inside = flag(p > lo) * flag(p < hi)
        def body(c, accs):
            hit = jnp.where(s16_ref[c] >= p16, one16, zero16)
            parts = hit.reshape(N_ACC16, TK // (N_ACC16 * PACKED_SUBLANES), PACKED_SUBLANES, TQ)
            return tuple(functools.reduce(lambda x, y: x + y, [a] + [parts[i, r] for r in range(parts.shape[1])])
                         for i, a in enumerate(accs))
        accs = lax.fori_loop(0, n_chunks, body, zero_accs16)
        c = finish(tuple(a.astype(jnp.float32) for a in accs))
        lo = jnp.where(inside * flag(c >= topk) > 0.0, p, lo)
        hi = jnp.where(inside * flag(c < topk) > 0.0, p, hi)
        return lo, hi

    one16 = jnp.ones((), jnp.bfloat16)
    zero16 = jnp.zeros((), jnp.bfloat16)
    N_ACC16 = 2
    zero_accs16 = tuple(jnp.zeros((PACKED_SUBLANES, TQ), jnp.bfloat16) for _ in range(N_ACC16))
    lo, hi_coarse = lax.fori_loop(0, COARSE_ITERS, coarse_step, (lo, hi))
    lo = lo - (jnp.abs(lo) * BF16_ULP_BOUND + F32_TINY)

    def probe(p, active, lo, hi, c_hi, thr, done):
        c = count_ge(p)
        is_eq = active * flag(c == topk)
        is_gt = active * flag(c > topk)
        is_lt = active * flag(c < topk)
        thr = jnp.where(is_eq > 0.0, p, thr)
        done = jnp.maximum(done, is_eq)
        lo = jnp.where(is_gt > 0.0, p, lo)
        hi = jnp.where(is_lt > 0.0, p, hi)
        c_hi = jnp.where(is_lt > 0.0, c, c_hi)
        return lo, hi, c_hi, thr, done

    def bisect_step(lo, hi, c_hi, thr, done, parked):
        p = 0.5 * lo + 0.5 * jnp.minimum(hi, smax)
        parked = jnp.maximum(parked, 1.0 - flag(p > lo) * flag(p < hi))
        parked = jnp.maximum(parked, flag(c_hi == topk - 1.0))
        active = flag(done + parked <= 0.0)
        return probe(p, active, lo, hi, c_hi, thr, done) + (parked,)

    lo, hi, c_hi, thr, done = probe(jnp.minimum(hi_coarse, smax), 1.0 - done, lo, hi, c_hi, thr, done)
    lo, hi, c_hi, thr, done, _ = lax.fori_loop(
        0, FINE_ITERS_UNCHECKED, lambda _, v: bisect_step(*v), (lo, hi, c_hi, thr, done, row(0.0)))

    def walk_step(hi, c_hi, thr, keep, done):
        def below_max(c, acc):
            blk = load_chunk(c)
            part = jnp.where(blk < hi, blk, neg_inf)
            return jnp.maximum(acc, jnp.max(part.reshape(TK // SUBLANES, SUBLANES, TQ), axis=0))
        m8 = lax.fori_loop(0, n_chunks, below_max, jnp.full((SUBLANES, TQ), neg_inf, jnp.float32))
        m = jnp.max(m8, axis=0, keepdims=True)
        c = count_ge(m)
        active = 1.0 - done
        reach = active * flag(c >= topk)
        thr = jnp.where(reach > 0.0, m, thr)
        keep = jnp.where(reach * flag(c > topk) > 0.0, topk - c_hi, keep)
        done = jnp.maximum(done, reach)
        step = active * (1.0 - reach)
        hi = jnp.where(step > 0.0, m, hi)
        c_hi = jnp.where(step > 0.0, c, c_hi)
        return hi, c_hi, thr, keep, done

    hi, c_hi, thr, keep, done = walk_step(hi, c_hi, thr, keep, done)

    def leftover_cond(st):
        return st[0] > 0.0

    def leftover_body(st):
        _, lo, hi, c_hi, thr, keep, done = st
        lo, hi, c_hi, thr, done, _ = bisect_step(*bisect_step(lo, hi, c_hi, thr, done, row(0.0)))
        hi, c_hi, thr, keep, done = walk_step(hi, c_hi, thr, keep, done)
        return (jnp.sum(1.0 - done), lo, hi, c_hi, thr, keep, done)

    st = (jnp.sum(1.0 - done), lo, hi, c_hi, thr, keep, done)
    _, _, _, _, thr, keep, _ = lax.while_loop(leftover_cond, leftover_body, st)

    has_ties = jnp.sum(flag(keep < KEEP_ALL)) > 0.0

    @pl.when(jnp.logical_not(has_ties))
    def _():
        def body(c, carry):
            store_chunk(c, jnp.where(load_chunk(c) >= thr, 0.0, neg_inf))
            return carry
        lax.fori_loop(0, n_chunks, body, 0)

    @pl.when(has_ties)
    def _():
        tri = flag(lax.broadcasted_iota(jnp.int32, (TK, TK), 1)
                   <= lax.broadcasted_iota(jnp.int32, (TK, TK), 0)).astype(jnp.bfloat16)
        def body(c, seen):
            blk = load_chunk(c)
            eq = flag(blk == thr)
            rank = seen + _dot(tri, eq.astype(jnp.bfloat16))
            sel = jnp.maximum(flag(blk > thr), eq * flag(rank <= keep))
            store_chunk(c, jnp.where(sel > 0.0, 0.0, neg_inf))
            return seen + finish(fold(eq))
        lax.fori_loop(0, n_chunks, body, row(0.0))

    m_ref[...] = jnp.full_like(m_ref, neg_inf)
    acc_ref[...] = jnp.zeros_like(acc_ref)

    last = n_chunks - 1

    def stage_head(h, c, bias, dst_ref):
        dst_ref[h] = bias + _dot(k_ref[0, h // Q_PER_KV, c], qT_ref[0, h * HEAD_DIM:(h + 1) * HEAD_DIM, :])

    def softmax_head(h, c, src_ref):
        m_old = m_ref[h]
        m_new = jnp.maximum(m_old, jnp.max(src_ref[h], axis=0, keepdims=True))
        m_safe = jnp.where(m_new == neg_inf, 0.0, m_new)
        corr = jnp.exp2(m_old - m_safe)
        p = jnp.exp2(src_ref[h] - m_safe)
        acc_ref[h] = acc_ref[h] * corr + _dot(vT_ref[0, h // Q_PER_KV, c], p.astype(jnp.bfloat16))
        m_ref[h] = m_new

    def step(c, src_ref, c_next, dst_ref):
        bias_next = scr_ref[c_next]
        for h in range(STAGE_LEAD):
            stage_head(h, c_next, bias_next, dst_ref)
        for h in range(N_HEADS):
            softmax_head(h, c, src_ref)
            if h + STAGE_LEAD < N_HEADS:
                stage_head(h + STAGE_LEAD, c_next, bias_next, dst_ref)

    odd = n_chunks % 2

    @pl.when(odd == 1)
    def _():
        bias0 = scr_ref[0]
        for h in range(N_HEADS):
            stage_head(h, 0, bias0, s0_ref)
        step(0, s0_ref, jnp.minimum(1, last), sa_ref)

    @pl.when(odd == 0)
    def _():
        bias0 = scr_ref[0]
        for h in range(N_HEADS):
            stage_head(h, 0, bias0, sa_ref)

    def attend_pair(i, carry):
        c0 = odd + 2 * i
        step(c0, sa_ref, c0 + 1, sb_ref)
        step(c0 + 1, sb_ref, jnp.minimum(c0 + 2, last), sa_ref)
        return carry

    lax.fori_loop(0, n_chunks // 2, attend_pair, 0)
    outT = jnp.concatenate(
        [acc_ref[h, :HEAD_DIM, :] / acc_ref[h, HEAD_DIM:HEAD_DIM + 1, :] for h in range(N_HEADS)], axis=0)
    out_ref[0] = outT.T.astype(jnp.bfloat16)


def _dsa_attention(qT, qiT, wiT, k, ki, vT):
    B, _, S = qT.shape
    k = k.reshape(B, N_KV_HEADS, S // TK, TK, HEAD_DIM)
    ki = ki.reshape(B, S // TK, TK, IDX_DIM)
    per_batch = lambda a: pl.BlockSpec((1,) + a.shape[1:], lambda b, i: (b,) + (0,) * (a.ndim - 1))
    return pl.pallas_call(
        _dsa_kernel,
        grid=(B, S // TQ),
        in_specs=[
            pl.BlockSpec((1, Q_COLS, TQ), lambda b, i: (b, 0, i)),
            pl.BlockSpec((1, QI_COLS, TQ), lambda b, i: (b, 0, i)),
            pl.BlockSpec((1, IDX_HEADS, TQ), lambda b, i: (b, 0, i)),
            per_batch(k), per_batch(ki), per_batch(vT),
        ],
        out_specs=pl.BlockSpec((1, TQ, Q_COLS), lambda b, i: (b, i, 0)),
        out_shape=jax.ShapeDtypeStruct((B, S, Q_COLS), jnp.bfloat16),
        scratch_shapes=[
            pltpu.VMEM((S // TK, TK, TQ), jnp.float32),
            pltpu.VMEM((S // TK, TK, TQ), jnp.bfloat16),
            pltpu.VMEM((N_HEADS, TK, TQ), jnp.float32),
            pltpu.VMEM((N_HEADS, TK, TQ), jnp.float32),
            pltpu.VMEM((N_HEADS, TK, TQ), jnp.float32),
            pltpu.VMEM((N_HEADS, 1, TQ), jnp.float32),
            pltpu.VMEM((N_HEADS, V_ROWS, TQ), jnp.float32),
        ],
        compiler_params=pltpu.CompilerParams(
            dimension_semantics=("arbitrary", "arbitrary"), vmem_limit_bytes=VMEM_LIMIT_BYTES),
        name="dsa_attention",
    )(qT, qiT, wiT, k, ki, vT)


def _layer_norm(z, g, b):
    mu = jnp.mean(z, axis=-1, keepdims=True)
    zc = z - mu
    var = jnp.mean(zc * zc, axis=-1, keepdims=True)
    return zc * lax.rsqrt(var + LN_EPS) * g + b


def _router_gates(x1, hi, wr_ref, br_ref):
    lo = (x1 - hi.astype(jnp.float32)).astype(jnp.bfloat16)

    r = _dot(hi, wr_ref[...])
    logits = r[:, :LANES] + r[:, LANES:] + _dot(lo, wr_ref[:, :LANES]) + br_ref[...]

    tm = logits.shape[0]
    lane = lax.broadcasted_iota(jnp.int32, (tm, LANES), 1)
    neg_inf = -jnp.inf
    is_grp = jnp.where(lane >= N_EXPERTS, jnp.where(lane < N_EXPERTS + N_EXPERT_GROUPS, 1.0, 0.0), 0.0)
    gl = jnp.where(is_grp > 0.0, logits, neg_inf)
    g_max = jnp.max(gl, axis=1, keepdims=True)
    g_sel = jnp.min(jnp.where(gl == g_max, lane, LANES), axis=1, keepdims=True) - N_EXPERTS
    g_weight = 1.0 / jnp.sum(jnp.exp(gl - g_max), axis=1, keepdims=True)

    in_grp = jnp.right_shift(lane, EXPERTS_PER_GROUP.bit_length() - 1) == g_sel
    el = jnp.where(in_grp, logits, neg_inf)
    v1 = jnp.max(el, axis=1, keepdims=True)
    i1 = jnp.min(jnp.where(el == v1, lane, LANES), axis=1, keepdims=True)
    el2 = jnp.where(lane == i1, neg_inf, el)
    v2 = jnp.max(el2, axis=1, keepdims=True)
    i2 = jnp.min(jnp.where(el2 == v2, lane, LANES), axis=1, keepdims=True)
    e2 = jnp.exp(v2 - v1)
    w1 = g_weight / (1.0 + e2)
    w2 = g_weight * e2 / (1.0 + e2)
    return jnp.where(lane == i1, w1, jnp.where(lane == i2, w2, 0.0)), g_sel


def _split_bf16(v, parts):
    out = []
    for _ in range(parts):
        piece = v.astype(jnp.bfloat16)
        out.append(piece)
        v = v - piece.astype(jnp.float32)
    return out


def _tail_kernel(attn_ref, pool_ref, x_ref, woa_ref, wop_ref, g1_ref, b1_ref, wr_ref, br_ref,
                 wg_ref, wu_ref, wd_ref, g2_ref, b2_ref, out_ref, hid_ref, ys_ref):
    f32, bf = jnp.float32, jnp.bfloat16
    mix = _dot(attn_ref[...], woa_ref[...]) + _dot(pool_ref[...], wop_ref[...])
    x1 = _layer_norm(ALPHA * x_ref[...] + mix, g1_ref[...], b1_ref[...])
    xb = x1.astype(bf)
    gate, g_sel = _router_gates(x1, xb, wr_ref, br_ref)
    tm = gate.shape[0]
    flag = lambda cond: jnp.where(cond, 1.0, 0.0)

    lane = lax.broadcasted_iota(jnp.int32, (tm, LANES), 1)
    onehot = flag(lane == g_sel)
    counts = jnp.sum(onehot, axis=0, keepdims=True)
    earlier = flag(lax.broadcasted_iota(jnp.int32, (tm, tm), 1)
                   < lax.broadcasted_iota(jnp.int32, (tm, tm), 0)).astype(bf)
    before = _dot(earlier, onehot.astype(bf))
    rank = jnp.sum(onehot * before, axis=1, keepdims=True)
    start = jnp.sum(jnp.where(lane < g_sel, counts, 0.0), axis=1, keepdims=True)
    pos = start + rank

    pos_hi = jnp.floor(pos * (1.0 / POS_RADIX))
    digits = jnp.where(lane == 0, pos_hi, jnp.where(lane == 1, pos - POS_RADIX * pos_hi, 0.0)).astype(bf)
    pick = flag(lax.broadcasted_iota(jnp.int32, (PACKED_SUBLANES, LANES), 0)
                == lax.broadcasted_iota(jnp.int32, (PACKED_SUBLANES, LANES), 1)).astype(bf)
    digits_row = _dot_nt(pick, digits)
    pos_row = POS_RADIX * digits_row[0:1] + digits_row[1:2]

    lane1 = lax.broadcasted_iota(jnp.int32, (1, LANES), 1)
    group_count = [jnp.sum(jnp.where(lane1 == g, counts, 0.0)) for g in range(N_EXPERT_GROUPS)]
    group_start = [jnp.sum(jnp.where(lane1 < g, counts, 0.0)) for g in range(N_EXPERT_GROUPS)]
    gate_parts = _split_bf16(gate, 3)
    lane_rt = lax.broadcasted_iota(jnp.int32, (ROW_TILE, LANES), 1)

    ys_ref[...] = jnp.zeros_like(ys_ref)
    for r in range(tm // ROW_TILE):
        row0 = r * ROW_TILE
        tile_rows = slice(row0, row0 + ROW_TILE)
        sorted_row = (row0 + lax.broadcasted_iota(jnp.int32, (ROW_TILE, tm), 0)).astype(f32)
        perm = flag(sorted_row == pos_row).astype(bf)
        xs = _dot(perm, xb).astype(bf)
        gate_s = functools.reduce(lambda a, b: a + b, [_dot(perm, p) for p in gate_parts])
        for g in range(N_EXPERT_GROUPS):
            occurs = jnp.logical_and(
                group_count[g] > 0.0,
                jnp.logical_and(group_start[g] < row0 + ROW_TILE, group_start[g] + group_count[g] > row0))

            @pl.when(occurs)
            def _(g=g, xs=xs, gate_s=gate_s, tile_rows=tile_rows):
                for e in range(EXPERTS_PER_GROUP):
                    ex = g * EXPERTS_PER_GROUP + e
                    a = _dot(xs, wg_ref[ex])
                    b = _dot(xs, wu_ref[ex])
                    gcol = jnp.sum(jnp.where(lane_rt == ex, gate_s, 0.0), axis=1, keepdims=True)
                    hid = (a * jax.nn.sigmoid(a)) * b * gcol
                    hid_ref[:, e * D_EXPERT:(e + 1) * D_EXPERT] = hid.astype(bf)
                rows = slice(g * EXPERTS_PER_GROUP * D_EXPERT, (g + 1) * EXPERTS_PER_GROUP * D_EXPERT)
                ys_ref[tile_rows, :] += _dot(hid_ref[...], wd_ref[rows, :])

    unperm = flag(lax.broadcasted_iota(jnp.int32, (tm, tm), 1).astype(f32) == pos).astype(bf)
    y = functools.reduce(lambda a, b: a + b, [_dot(unperm, piece) for piece in _split_bf16(ys_ref[...], 2)])
    out_ref[...] = _layer_norm(ALPHA * x1 + y, g2_ref[...], b2_ref[...])


def _output_and_experts(attn, pool, x2d, w_out, ln1_g, ln1_b, w_gr, b_gr, w_er, b_er,
                        w_gate, w_up, w_down, ln2_g, ln2_b):
    T, D = x2d.shape
    tm = TM_TAIL
    bf = jnp.bfloat16
    woa = w_out[:Q_COLS].astype(bf)
    wop = w_out[Q_COLS:].astype(bf)
    w_r = jnp.pad(jnp.concatenate([w_er, w_gr], axis=1), ((0, 0), (0, LANES - N_EXPERTS - N_EXPERT_GROUPS)))
    w_r_hi = w_r.astype(bf)
    w_r_lo = (w_r - w_r_hi.astype(jnp.float32)).astype(bf)
    wr = jnp.concatenate([w_r_hi, w_r_lo], axis=1)
    br = jnp.pad(jnp.concatenate([b_er, b_gr]), (0, LANES - N_EXPERTS - N_EXPERT_GROUPS)).reshape(1, LANES)
    wg = w_gate.reshape(N_EXPERTS, D, D_EXPERT).astype(bf)
    wu = w_up.reshape(N_EXPERTS, D, D_EXPERT).astype(bf)
    wd = w_down.reshape(N_EXPERTS * D_EXPERT, D).astype(bf)
    resident = lambda a: pl.BlockSpec(a.shape, lambda i: (0,) * a.ndim, pipeline_mode=pl.Buffered(1))
    consts = (woa, wop, ln1_g.reshape(1, D), ln1_b.reshape(1, D), wr, br,
              wg, wu, wd, ln2_g.reshape(1, D), ln2_b.reshape(1, D))
    return pl.pallas_call(
        _tail_kernel,
        grid=(T // tm,),
        in_specs=[pl.BlockSpec((tm, Q_COLS), lambda i: (i, 0)),
                  pl.BlockSpec((tm, D_POOL), lambda i: (i, 0)),
                  pl.BlockSpec((tm, D), lambda i: (i, 0))] + [resident(c) for c in consts],
        out_specs=pl.BlockSpec((tm, D), lambda i: (i, 0)),
        out_shape=jax.ShapeDtypeStruct((T, D), jnp.float32),
        scratch_shapes=[pltpu.VMEM((ROW_TILE, EXPERTS_PER_GROUP * D_EXPERT), bf),
                        pltpu.VMEM((tm, D), jnp.float32)],
        compiler_params=pltpu.CompilerParams(
            dimension_semantics=("arbitrary",), vmem_limit_bytes=VMEM_LIMIT_BYTES),
        name="dsa_output_and_experts",
    )(attn, pool, x2d, *consts)


def kernel(x, w_in, w_pool, pool_scale, w_out, ln1_g, ln1_b, w_group_router, b_group_router,
           w_expert_router, b_expert_router, w_gate, w_up, w_down, ln2_g, ln2_b):
    B, S, D = x.shape
    assert D == D_MODEL and S % TM_PROJ == 0 and S % TQ == 0 and TQ == TK
    assert w_in.shape[0] == DEPTH == 1
    l = 0
    qT, vT, qiT, wiT, k, ki, pool = _input_projection(x, w_in[l], w_pool[l], pool_scale[l])
    attn = _dsa_attention(qT, qiT, wiT, k, ki, vT)
    x2d = x.reshape(B * S, D)
    out = _output_and_experts(
        attn.reshape(B * S, Q_COLS), pool.reshape(B * S, D_POOL), x2d, w_out[l], ln1_g[l], ln1_b[l],
        w_group_router[l], b_group_router[l], w_expert_router[l], b_expert_router[l],
        w_gate[l], w_up[l], w_down[l], ln2_g[l], ln2_b[l])
    return out.reshape(B, S, D)
```
